```python
import math
import jax
import jax.numpy as jnp
from jax import lax
import numpy as np

D_MODEL = 2048
BATCH = 8
SEQ = 2048
DEPTH = 2

SSD_HEADS = 32
SSD_HEAD_DIM = 64
SSD_WIDTH = SSD_HEADS * SSD_HEAD_DIM
SSD_GROUPS = 4
SSD_STATE = 128
SSD_CONV = 4
SSD_CHUNK = 128
CONV_CH = SSD_WIDTH + 2 * SSD_GROUPS * SSD_STATE
MOBA_HEADS = 16
MOBA_HEAD_DIM = 128
MOBA_WIDTH = MOBA_HEADS * MOBA_HEAD_DIM
MOBA_BLOCK = 256
MOBA_TOPK = 3
MOBA_QCHUNK = 128
ROPE_THETA = 10000.0
GLA_HEADS = 4
GLA_KEY_DIM = D_MODEL // 2
GLA_VAL_DIM = D_MODEL
GLA_DK = GLA_KEY_DIM // GLA_HEADS
GLA_DV = GLA_VAL_DIM // GLA_HEADS
GLA_GATE_RANK = 16
GLA_GATE_NORM = 16.0
GLA_CHUNK = 64
N_EXPERTS = 32
TOP_K = 4
D_FF = D_MODEL
SWIGLU_LIMIT = 7.0
SWIGLU_ALPHA = 1.702
MOE_BLOCK = 256
DN_ALPHA = (2 * DEPTH) ** 0.25
DN_BETA = (8 * DEPTH) ** -0.25
EPS = 1e-5
N_EVEN = (DEPTH + 1) // 2
N_ODD = DEPTH // 2

HYB_SPLITS = [SSD_WIDTH,
              SSD_WIDTH + CONV_CH,
              SSD_WIDTH + CONV_CH + SSD_HEADS,
              SSD_WIDTH + CONV_CH + SSD_HEADS + MOBA_WIDTH,
              SSD_WIDTH + CONV_CH + SSD_HEADS + 2 * MOBA_WIDTH]
HYB_IN = HYB_SPLITS[-1] + MOBA_WIDTH
GLA_SPLITS = [GLA_KEY_DIM, 2 * GLA_KEY_DIM, 2 * GLA_KEY_DIM + GLA_VAL_DIM,
              2 * GLA_KEY_DIM + 2 * GLA_VAL_DIM]
GLA_IN = GLA_SPLITS[-1] + GLA_GATE_RANK

kernel_name = "hybrid_ssd_moba_gla_moe_deepnorm"


def layer_norm(x, g, b):
    xf = x.astype(jnp.float32)
    mu = jnp.mean(xf, axis=-1, keepdims=True)
    var = jnp.mean(jnp.square(xf - mu), axis=-1, keepdims=True)
    return ((xf - mu) * lax.rsqrt(var + EPS) * g + b).astype(x.dtype)


def rope(t, pos):
    half = t.shape[-1] // 2
    inv = jnp.exp(-math.log(ROPE_THETA) * jnp.arange(half, dtype=jnp.float32) / half)
    ang = pos.astype(jnp.float32)[:, None] * inv[None, :]
    cos = jnp.cos(ang)[None, :, None, :]
    sin = jnp.sin(ang)[None, :, None, :]
    tf = t.astype(jnp.float32)
    t1, t2 = tf[..., :half], tf[..., half:]
    return jnp.concatenate([t1 * cos - t2 * sin, t2 * cos + t1 * sin], axis=-1).astype(t.dtype)


def ssd_chunked(x, a, b, c):
    bsz, s, g, r, p = x.shape
    n = b.shape[-1]
    l = SSD_CHUNK
    nc = s // l
    x = x.reshape(bsz, nc, l, g, r, p)
    b = b.reshape(bsz, nc, l, g, n)
    c = c.reshape(bsz, nc, l, g, n)
    a = a.reshape(bsz, nc, l, g, r).transpose(0, 1, 3, 4, 2)
    a_cum = jnp.cumsum(a, axis=-1)
    causal = jnp.tril(jnp.ones((l, l), dtype=bool))
    seg = a_cum[..., :, None] - a_cum[..., None, :]
    decay = jnp.where(causal, jnp.exp(jnp.where(causal, seg, 0.0)), 0.0)
    cb = jnp.einsum('bclgn,bcsgn->bcgls', c, b)
    y_diag = jnp.einsum('bcgrls,bcsgrp->bclgrp', cb[:, :, :, None] * decay, x)
    decay_to_end = jnp.exp(a_cum[..., -1:] - a_cum).transpose(0, 1, 4, 2, 3)
    states = jnp.einsum('bclgn,bclgrp->bcgrpn', b, x * decay_to_end[..., None])
    chunk_decay = jnp.exp(a_cum[..., -1])

    def step(h, inp):
        s_c, d_c = inp
        return h * d_c[..., None, None] + s_c, h

    h0 = jnp.zeros((bsz, g, r, p, n), states.dtype)
    _, prev = lax.scan(step, h0, (jnp.moveaxis(states, 1, 0), jnp.moveaxis(chunk_decay, 1, 0)))
    prev = jnp.moveaxis(prev, 0, 1)
    y_off = jnp.einsum('bclgn,bcgrpn->bclgrp', c, prev) * jnp.exp(a_cum).transpose(0, 1, 4, 2, 3)[..., None]
    return (y_diag + y_off).reshape(bsz, s, g * r, p)


def moba_attention(q, k, v):
    bsz, s, h, dh = q.shape
    nb = -(-s // MOBA_BLOCK)
    sp = nb * MOBA_BLOCK
    nq = sp // MOBA_QCHUNK
    k_sel = min(MOBA_TOPK, nb - 1)
    scale = dh ** -0.5

    def to_bh(t):
        t = jnp.pad(t, ((0, 0), (0, sp - s), (0, 0), (0, 0)))
        return t.transpose(0, 2, 1, 3).reshape(bsz * h, sp, dh)

    qb, kb, vb = to_bh(q), to_bh(k), to_bh(v)
    kblk = kb.reshape(bsz * h, nb, MOBA_BLOCK, dh)
    vblk = vb.reshape(bsz * h, nb, MOBA_BLOCK, dh)
    qblock = jnp.arange(sp) // MOBA_BLOCK
    n_units = bsz * h * nq
    q_units = qb.reshape(n_units, MOBA_QCHUNK, dh)
    unit = jnp.arange(n_units)
    unit_bh = unit // nq
    unit_chunk = unit % nq

    if k_sel > 0:
        kmean = jnp.mean(kblk.astype(jnp.float32), axis=2)
        gate = jnp.einsum('utd,ujd->utj', qb.astype(jnp.float32), kmean)
        past = jnp.arange(nb)[None, :] < qblock[:, None]
        gate = jnp.where(past, gate, -jnp.inf)
        _, idx = lax.top_k(gate, k_sel)
        ok = jnp.broadcast_to(jnp.arange(k_sel)[None, :] < qblock[:, None], idx.shape)
        idx = jnp.where(ok, idx, 0).reshape(n_units, MOBA_QCHUNK, k_sel)
        ok = ok.reshape(n_units, MOBA_QCHUNK, k_sel)
        xs = (q_units, unit_bh, unit_chunk, idx, ok)
    else:
        xs = (q_units, unit_bh, unit_chunk)

    def attend(args):
        q_u, bh, ch = args[0], args[1], args[2]
        kb_u = kblk[bh]
        vb_u = vblk[bh]
        own = (ch * MOBA_QCHUNK) // MOBA_BLOCK
        k_own = kb_u[own]
        v_own = vb_u[own]
        qp = ch * MOBA_QCHUNK + jnp.arange(MOBA_QCHUNK)
        kp = own * MOBA_BLOCK + jnp.arange(MOBA_BLOCK)
        s_own = (q_u @ k_own.T).astype(jnp.float32) * scale
        s_own = jnp.where(kp[None, :] <= qp[:, None], s_own, -jnp.inf)
        if k_sel > 0:
            idx_u, ok_u = args[3], args[4]
            k_g = kb_u[idx_u]
            v_g = vb_u[idx_u]
            s_sel = jnp.einsum('qd,qjkd->qjk', q_u, k_g).astype(jnp.float32) * scale
            s_sel = jnp.where(ok_u[:, :, None], s_sel, -jnp.inf).reshape(MOBA_QCHUNK, k_sel * MOBA_BLOCK)
            p = jax.nn.softmax(jnp.concatenate([s_own, s_sel], axis=-1), axis=-1).astype(v_own.dtype)
            p_own = p[:, :MOBA_BLOCK]
            p_sel = p[:, MOBA_BLOCK:].reshape(MOBA_QCHUNK, k_sel, MOBA_BLOCK)
            return p_own @ v_own + jnp.einsum('qjk,qjkd->qd', p_sel, v_g)
        p = jax.nn.softmax(s_own, axis=-1).astype(v_own.dtype)
        return p @ v_own

    out = lax.map(attend, xs)
    out = out.reshape(bsz, h, sp, dh)[:, :, :s]
    return out.transpose(0, 2, 1, 3)


def hybrid_ssd_moba(h, w_in, conv_w, conv_b, dt_bias, a_log, d_skip, ssm_norm, w_out):
    bsz, s, _ = h.shape
    f32 = jnp.float32
    proj = h @ w_in
    z, xbc, dt, q, k, v = jnp.split(proj, HYB_SPLITS, axis=-1)
    xbc = lax.conv_general_dilated(xbc, conv_w[:, None, :].astype(xbc.dtype), window_strides=(1,),
                                   padding=[(SSD_CONV - 1, 0)],
                                   dimension_numbers=('NWC', 'WIO', 'NWC'),
                                   feature_group_count=CONV_CH)
    xbc = jax.nn.silu(xbc + conv_b)
    xs, bs, cs = jnp.split(xbc, [SSD_WIDTH, SSD_WIDTH + SSD_GROUPS * SSD_STATE], axis=-1)
    r = SSD_HEADS // SSD_GROUPS
    dt = jax.nn.softplus((dt + dt_bias).astype(f32))
    a = -jnp.exp(a_log.astype(f32))
    xh = xs.astype(f32).reshape(bsz, s, SSD_GROUPS, r, SSD_HEAD_DIM)
    dth = dt.reshape(bsz, s, SSD_GROUPS, r)
    y = ssd_chunked(xh * dth[..., None], a.reshape(SSD_GROUPS, r) * dth,
                    bs.astype(f32).reshape(bsz, s, SSD_GROUPS, SSD_STATE),
                    cs.astype(f32).reshape(bsz, s, SSD_GROUPS, SSD_STATE))
    y = y + d_skip.astype(f32)[:, None] * xh.reshape(bsz, s, SSD_HEADS, SSD_HEAD_DIM)
    y = y.reshape(bsz, s, SSD_WIDTH) * jax.nn.silu(z.astype(f32))
    yg = y.reshape(bsz, s, SSD_GROUPS, SSD_WIDTH // SSD_GROUPS)
    yg = yg * lax.rsqrt(jnp.mean(jnp.square(yg), axis=-1, keepdims=True) + EPS)
    y_ssd = (yg.reshape(bsz, s, SSD_WIDTH) * ssm_norm).astype(h.dtype)
    pos = jnp.arange(s)
    q = rope(q.reshape(bsz, s, MOBA_HEADS, MOBA_HEAD_DIM), pos)
    k = rope(k.reshape(bsz, s, MOBA_HEADS, MOBA_HEAD_DIM), pos)
    v = v.reshape(bsz, s, MOBA_HEADS, MOBA_HEAD_DIM)
    y_att = moba_attention(q, k, v).reshape(bsz, s, MOBA_WIDTH).astype(h.dtype)
    return jnp.concatenate([y_ssd, y_att], axis=-1) @ w_out


def gla_chunked(q, k, v, log_a):
    bsz, h, s, dk = q.shape
    dv = v.shape[-1]
    l = GLA_CHUNK
    nc = s // l

    def chunks(t):
        return jnp.moveaxis(t.reshape(bsz, h, nc, l, t.shape[-1]), 2, 0)

    g_cum = jnp.cumsum(chunks(log_a), axis=-2)
    g_last = g_cum[..., -1:, :]
    qc, kc, vc = chunks(q), chunks(k), chunks(v)
    q_in = qc * jnp.exp(g_cum)
    k_in = kc * jnp.exp(-g_cum)
    k_end = kc * jnp.exp(g_last - g_cum)
    causal = jnp.tril(jnp.ones((l, l), dtype=bool))

    def step(state, inp):
        qi, ki, ke, vi, gl = inp
        attn = jnp.where(causal, jnp.einsum('bhid,bhjd->bhij', qi, ki), 0.0)
        o = jnp.einsum('bhij,bhjv->bhiv', attn, vi) + jnp.einsum('bhid,bhdv->bhiv', qi, state)
        state = state * jnp.exp(gl)[:, :, 0, :, None] + jnp.einsum('bhjd,bhjv->bhdv', ke, vi)
        return state, o

    state0 = jnp.zeros((bsz, h, dk, dv), jnp.float32)
    _, o = lax.scan(step, state0, (q_in, k_in, k_end, vc, g_last))
    return jnp.moveaxis(o, 0, 2).reshape(bsz, h, s, dv)


def gla_mixer(h, w_in, w_gate2, b_gate, head_norm, w_out):
    bsz, s, _ = h.shape
    f32 = jnp.float32
    proj = h @ w_in
    q, k, v, g, gl = jnp.split(proj, GLA_SPLITS, axis=-1)
    log_a = jax.nn.log_sigmoid((gl @ w_gate2 + b_gate).astype(f32)) / GLA_GATE_NORM

    def heads(t, d):
        return t.astype(f32).reshape(bsz, s, GLA_HEADS, d).transpose(0, 2, 1, 3)

    o = gla_chunked(heads(q, GLA_DK) * GLA_DK ** -0.5, heads(k, GLA_DK),
                    heads(v, GLA_DV), heads(log_a, GLA_DK))
    o = o * lax.rsqrt(jnp.mean(jnp.square(o), axis=-1, keepdims=True) + EPS) * head_norm
    o = o.transpose(0, 2, 1, 3).reshape(bsz, s, GLA_VAL_DIM) * jax.nn.silu(g.astype(f32))
    return o.astype(h.dtype) @ w_out


def moe_ffn(h, w_router, b_router, w1, b1, w2, b2):
    bsz, s, d = h.shape
    t = bsz * s
    xf = h.reshape(t, d)
    logits = (xf @ w_router).astype(jnp.float32) + b_router
    top_val, top_e = lax.top_k(logits, TOP_K)
    gate = jax.nn.softmax(top_val, axis=-1)
    e_flat = top_e.reshape(-1)
    tok_flat = jnp.repeat(jnp.arange(t, dtype=jnp.int32), TOP_K)
    g_flat = gate.reshape(-1)
    order = jnp.argsort(e_flat)
    e_s, tok_s, g_s = e_flat[order], tok_flat[order], g_flat[order]
    counts = jnp.bincount(e_flat, length=N_EXPERTS)
    padded = (counts + MOE_BLOCK - 1) // MOE_BLOCK * MOE_BLOCK
    start = jnp.cumsum(counts) - counts
    pend = jnp.cumsum(padded)
    pstart = pend - padded
    slot = pstart[e_s] + jnp.arange(t * TOP_K) - start[e_s]
    n_blocks = -(-(t * TOP_K) // MOE_BLOCK) + N_EXPERTS
    rows = n_blocks * MOE_BLOCK
    tok_pad = jnp.full((rows,), t, jnp.int32).at[slot].set(tok_s)
    g_pad = jnp.zeros((rows,), jnp.float32).at[slot].set(g_s)
    block_e = jnp.clip(jnp.searchsorted(pend, jnp.arange(n_blocks) * MOE_BLOCK, side='right'),
                       0, N_EXPERTS - 1)
    x_pad = jnp.concatenate([xf, jnp.zeros((1, d), xf.dtype)], axis=0)

    def expert_block(args):
        tok_b, e = args
        hb = x_pad[tok_b] @ w1[e] + b1[e]
        glu = jnp.minimum(hb[:, ::2], SWIGLU_LIMIT)
        lin = jnp.clip(hb[:, 1::2], -SWIGLU_LIMIT, SWIGLU_LIMIT)
        act = glu * jax.nn.sigmoid(SWIGLU_ALPHA * glu) * (lin + 1.0)
        return act @ w2[e] + b2[e]

    y = lax.map(expert_block, (tok_pad.reshape(n_blocks, MOE_BLOCK), block_e))
    y = y.reshape(rows, d) * g_pad[:, None].astype(y.dtype)
    out = jnp.zeros((t + 1, d), y.dtype).at[tok_pad].add(y)[:t]
    return out.reshape(bsz, s, d).astype(h.dtype)


def setup_inputs(seed: int = 0) -> dict:
    key = jax.random.key(seed)
    ks = jax.random.split(key, 24)
    f32 = jnp.float32

    def nrm(k, shape, scale):
        return jax.random.normal(k, shape, f32) * scale

    x = nrm(ks[0], (BATCH, SEQ, D_MODEL), 1.0)
    hyb_w_in = nrm(ks[1], (N_EVEN, D_MODEL, HYB_IN), D_MODEL ** -0.5)
    hyb_conv_w = nrm(ks[2], (N_EVEN, SSD_CONV, CONV_CH), SSD_CONV ** -0.5)
    hyb_conv_b = nrm(ks[3], (N_EVEN, CONV_CH), 0.01)
    dt0 = jnp.exp(jax.random.uniform(ks[4], (N_EVEN, SSD_HEADS), f32, math.log(1e-3), math.log(1e-1)))
    hyb_dt_bias = dt0 + jnp.log(-jnp.expm1(-dt0))
    hyb_a_log = jnp.log(jax.random.uniform(ks[5], (N_EVEN, SSD_HEADS), f32, 1.0, 16.0))
    hyb_d = 1.0 + nrm(ks[6], (N_EVEN, SSD_HEADS), 0.01)
    hyb_norm = 1.0 + nrm(ks[7], (N_EVEN, SSD_WIDTH), 0.01)
    hyb_w_out = nrm(ks[8], (N_EVEN, SSD_WIDTH + MOBA_WIDTH, D_MODEL),
                    (SSD_WIDTH + MOBA_WIDTH) ** -0.5 * DN_BETA)
    gla_w_in = nrm(ks[9], (N_ODD, D_MODEL, GLA_IN), D_MODEL ** -0.5)
    gla_w_gate2 = nrm(ks[10], (N_ODD, GLA_GATE_RANK, GLA_KEY_DIM), GLA_GATE_RANK ** -0.5)
    gla_b_gate = nrm(ks[11], (N_ODD, GLA_KEY_DIM), 0.01)
    gla_norm = 1.0 + nrm(ks[12], (N_ODD, GLA_DV), 0.01)
    gla_w_out = nrm(ks[13], (N_ODD, GLA_VAL_DIM, D_MODEL), GLA_VAL_DIM ** -0.5 * DN_BETA)
    ln1_g = 1.0 + nrm(ks[14], (DEPTH, D_MODEL), 0.01)
    ln1_b = nrm(ks[15], (DEPTH, D_MODEL), 0.01)
    ln2_g = 1.0 + nrm(ks[16], (DEPTH, D_MODEL), 0.01)
    ln2_b = nrm(ks[17], (DEPTH, D_MODEL), 0.01)
    moe_w_router = nrm(ks[18], (DEPTH, D_MODEL, N_EXPERTS), D_MODEL ** -0.5)
    moe_b_router = nrm(ks[19], (DEPTH, N_EXPERTS), 0.01)
    moe_w1 = nrm(ks[20], (DEPTH, N_EXPERTS, D_MODEL, 2 * D_FF), D_MODEL ** -0.5)
    moe_b1 = nrm(ks[21], (DEPTH, N_EXPERTS, 2 * D_FF), 0.01)
    moe_w2 = nrm(ks[22], (DEPTH, N_EXPERTS, D_FF, D_MODEL), D_FF ** -0.5 * DN_BETA)
    moe_b2 = nrm(ks[23], (DEPTH, N_EXPERTS, D_MODEL), 0.01)
    return {"x": x, "hyb_w_in": hyb_w_in, "hyb_conv_w": hyb_conv_w, "hyb_conv_b": hyb_conv_b,
            "hyb_dt_bias": hyb_dt_bias, "hyb_a_log": hyb_a_log, "hyb_d": hyb_d,
            "hyb_norm": hyb_norm, "hyb_w_out": hyb_w_out, "gla_w_in": gla_w_in,
            "gla_w_gate2": gla_w_gate2, "gla_b_gate": gla_b_gate, "gla_norm": gla_norm,
            "gla_w_out": gla_w_out, "ln1_g": ln1_g, "ln1_b": ln1_b, "ln2_g": ln2_g,
            "ln2_b": ln2_b, "moe_w_router": moe_w_router, "moe_b_router": moe_b_router,
            "moe_w1": moe_w1, "moe_b1": moe_b1, "moe_w2": moe_w2, "moe_b2": moe_b2}


def reference(x, hyb_w_in, hyb_conv_w, hyb_conv_b, hyb_dt_bias, hyb_a_log, hyb_d, hyb_norm,
              hyb_w_out, gla_w_in, gla_w_gate2, gla_b_gate, gla_norm, gla_w_out,
              ln1_g, ln1_b, ln2_g, ln2_b, moe_w_router, moe_b_router, moe_w1, moe_b1,
              moe_w2, moe_b2):
    h = x
    for layer in range(DEPTH):
        j = layer // 2
        if layer % 2 == 0:
            mix = hybrid_ssd_moba(h, hyb_w_in[j], hyb_conv_w[j], hyb_conv_b[j], hyb_dt_bias[j],
                                  hyb_a_log[j], hyb_d[j], hyb_norm[j], hyb_w_out[j])
        else:
            mix = gla_mixer(h, gla_w_in[j], gla_w_gate2[j], gla_b_gate[j], gla_norm[j], gla_w_out[j])
        h = layer_norm(DN_ALPHA * h + mix, ln1_g[layer], ln1_b[layer])
        ffn = moe_ffn(h, moe_w_router[layer], moe_b_router[layer], moe_w1[layer], moe_b1[layer],
                      moe_w2[layer], moe_b2[layer])
        h = layer_norm(DN_ALPHA * h + ffn, ln2_g[layer], ln2_b[layer])
    return h
```

```python
import functools
import math

import jax
import jax.numpy as jnp
from jax import lax
from jax.experimental import pallas as pl
from jax.experimental.pallas import tpu as pltpu

F32 = jnp.float32
BF16 = jnp.bfloat16
I32 = jnp.int32
HIGHEST = lax.Precision.HIGHEST

D_MODEL = 2048
DEPTH = 2
SSD_HEADS = 32
SSD_HEAD_DIM = 64
SSD_WIDTH = SSD_HEADS * SSD_HEAD_DIM
SSD_GROUPS = 4
SSD_STATE = 128
SSD_CONV = 4
SSD_CHUNK = 128
CONV_CH = SSD_WIDTH + 2 * SSD_GROUPS * SSD_STATE
GROUP_W = SSD_WIDTH // SSD_GROUPS
MOBA_HEADS = 16
MOBA_HEAD_DIM = 128
MOBA_WIDTH = MOBA_HEADS * MOBA_HEAD_DIM
MOBA_BLOCK = 256
MOBA_TOPK = 3
ROPE_THETA = 10000.0
GLA_HEADS = 4
GLA_KEY_DIM = D_MODEL // 2
GLA_VAL_DIM = D_MODEL
GLA_DK = GLA_KEY_DIM // GLA_HEADS
GLA_DV = GLA_VAL_DIM // GLA_HEADS
GLA_GATE_RANK = 16
GLA_GATE_NORM = 16.0
GLA_CHUNK = 64
GLA_ROWS = 512
N_EXPERTS = 32
TOP_K = 4
D_FF = D_MODEL
SWIGLU_LIMIT = 7.0
SWIGLU_ALPHA = 1.702
DN_ALPHA = (2 * DEPTH) ** 0.25
EPS = 1e-5

LANES = 128
HALO = 8
NEG_BIG = -1e30
VMEM_LIMIT = 56 * 1024 * 1024


def _cparams(n_axes, vmem=VMEM_LIMIT):
    return pltpu.CompilerParams(dimension_semantics=("arbitrary",) * n_axes, vmem_limit_bytes=vmem)


def _split3(x):
    hi = x.astype(BF16)
    r = x - hi.astype(F32)
    mid = r.astype(BF16)
    lo = (r - mid.astype(F32)).astype(BF16)
    return lo, mid, hi


def _dot_rhs01(x, e):
    out = None
    for p in _split3(x):
        t = jnp.dot(p, e, preferred_element_type=F32)
        out = t if out is None else out + t
    return out


def _dot_lhs01(t01, x):
    out = None
    for p in _split3(x):
        t = jnp.dot(t01, p, preferred_element_type=F32)
        out = t if out is None else out + t
    return out


def _dot_nt(a, b):
    return lax.dot_general(a, b, (((1,), (1,)), ((), ())), preferred_element_type=F32)


def _silu(x):
    return x * jax.nn.sigmoid(x)


def _softplus(x):
    return jnp.maximum(x, 0.0) + jnp.log1p(jnp.exp(-jnp.abs(x)))


def _mm_kernel(x_ref, w_ref, o_ref, wb_ref):
    @pl.when(pl.program_id(1) == 0)
    def _():
        wb_ref[...] = w_ref[...].astype(BF16)

    o_ref[...] = jnp.dot(x_ref[...], wb_ref[...], preferred_element_type=F32).astype(o_ref.dtype)


def matmul(x, w, *, n_cols, col_off=0, tm=1024, tn=512, out_dtype=F32):
    m, k = x.shape
    tm = min(tm, m)
    tn = min(tn, n_cols)
    assert m % tm == 0 and n_cols % tn == 0 and col_off % tn == 0
    joff = col_off // tn
    return pl.pallas_call(
        _mm_kernel,
        grid=(n_cols // tn, m // tm),
        in_specs=[pl.BlockSpec((tm, k), lambda j, i: (i, 0)),
                  pl.BlockSpec((k, tn), lambda j, i: (0, j + joff))],
        out_specs=pl.BlockSpec((tm, tn), lambda j, i: (i, j)),
        out_shape=jax.ShapeDtypeStruct((m, n_cols), out_dtype),
        scratch_shapes=[pltpu.VMEM((k, tn), BF16)],
        compiler_params=_cparams(2),
        name="dense_matmul",
    )(x, w)


def _ssd_kernel(z_ref, xbc_ref, dt_ref, cw_ref, cb_ref, dtb_ref, alog_ref, dfull_ref, norm_ref, e_ref,
                o_ref, halo_ref, state_ref):
    L = SSD_CHUNK
    c = pl.program_id(1)

    @pl.when(c == 0)
    def _():
        halo_ref[...] = jnp.zeros_like(halo_ref)
        state_ref[...] = jnp.zeros_like(state_ref)

    xbc = xbc_ref[...]
    ext = jnp.concatenate([halo_ref[...], xbc], axis=0)
    cw = cw_ref[...]
    conv = cb_ref[...] + cw[0:1, :] * ext[HALO - 3:HALO - 3 + L, :]
    for j in range(1, SSD_CONV):
        conv = conv + cw[j:j + 1, :] * ext[HALO - 3 + j:HALO - 3 + j + L, :]
    halo_ref[...] = xbc[L - HALO:, :]
    act = _silu(conv)
    xs = act[:, :SSD_WIDTH]

    dt = _softplus(dt_ref[...] + dtb_ref[...])
    adt = -jnp.exp(alog_ref[...]) * dt
    row = lax.broadcasted_iota(I32, (L, L), 0)
    col = lax.broadcasted_iota(I32, (L, L), 1)
    causal = row >= col
    tri = jnp.where(causal, 1.0, 0.0).astype(BF16)
    acum = _dot_lhs01(tri, adt)
    acum_t = acum.T
    e = e_ref[...]
    a_full = _dot_rhs01(acum, e)
    dt_full = _dot_rhs01(dt, e)
    xdt = xs * dt_full
    a_last = a_full[L - 1:L, :]
    exp_a = jnp.exp(a_full)
    chunk_decay = jnp.exp(a_last)
    xdt_b = xdt.astype(BF16)
    xe_b = (xdt * jnp.exp(a_last - a_full)).astype(BF16)
    lane = lax.broadcasted_iota(I32, (L, LANES), 1)

    ys = []
    for g in range(SSD_GROUPS):
        gs = slice(g * GROUP_W, (g + 1) * GROUP_W)
        bg = act[:, SSD_WIDTH + g * SSD_STATE:SSD_WIDTH + (g + 1) * SSD_STATE]
        cg = act[:, SSD_WIDTH + (SSD_GROUPS + g) * SSD_STATE:SSD_WIDTH + (SSD_GROUPS + g + 1) * SSD_STATE]
        bg_b = bg.astype(BF16)
        cg_b = cg.astype(BF16)
        cbm = _dot_nt(cg_b, bg_b)
        st = state_ref[g]
        y_off = jnp.dot(cg_b, st.astype(BF16), preferred_element_type=F32) * exp_a[:, gs]
        state_ref[g] = st * chunk_decay[:, gs] + jnp.dot(bg.T.astype(BF16), xe_b[:, gs],
                                                         preferred_element_type=F32)
        pieces = []
        for pr in range(GROUP_W // LANES):
            xp = xdt_b[:, g * GROUP_W + pr * LANES:g * GROUP_W + (pr + 1) * LANES]
            res = []
            for hh in range(2):
                h = g * (SSD_HEADS // SSD_GROUPS) + pr * 2 + hh
                seg = acum[:, h:h + 1] - acum_t[h:h + 1, :]
                dec = jnp.where(causal, jnp.exp(jnp.where(causal, seg, 0.0)), 0.0)
                res.append(jnp.dot((cbm * dec).astype(BF16), xp, preferred_element_type=F32))
            pieces.append(jnp.where(lane < SSD_HEAD_DIM, res[0], res[1]))
        ys.append(jnp.concatenate(pieces, axis=1) + y_off)
    y = jnp.concatenate(ys, axis=1) + dfull_ref[...] * xs
    y = y * _silu(z_ref[...])
    outs = []
    for g in range(SSD_GROUPS):
        yg = y[:, g * GROUP_W:(g + 1) * GROUP_W]
        outs.append(yg * lax.rsqrt(jnp.mean(yg * yg, axis=-1, keepdims=True) + EPS))
    o_ref[...] = (jnp.concatenate(outs, axis=1) * norm_ref[...]).astype(o_ref.dtype)


def ssd_mixer(z, xbc, dt, conv_w, conv_b, dt_bias, a_log, d_skip, ssm_norm, bsz, seq):
    t = bsz * seq
    nc = seq // SSD_CHUNK
    pad = LANES - SSD_HEADS
    dtb = jnp.pad(dt_bias.astype(F32), (0, pad)).reshape(1, LANES)
    alog = jnp.pad(a_log.astype(F32), (0, pad)).reshape(1, LANES)
    dfull = jnp.repeat(d_skip.astype(F32), SSD_HEAD_DIM).reshape(1, SSD_WIDTH)
    expand = (jnp.arange(LANES)[:, None] == (jnp.arange(SSD_WIDTH) // SSD_HEAD_DIM)[None, :]).astype(BF16)
    row_map = lambda b, c: (b * nc + c, 0)
    const = lambda b, c: (0, 0)
    return pl.pallas_call(
        _ssd_kernel,
        grid=(bsz, nc),
        in_specs=[pl.BlockSpec((SSD_CHUNK, SSD_WIDTH), row_map),
                  pl.BlockSpec((SSD_CHUNK, CONV_CH), row_map),
                  pl.BlockSpec((SSD_CHUNK, LANES), row_map),
                  pl.BlockSpec((SSD_CONV, CONV_CH), const),
                  pl.BlockSpec((1, CONV_CH), const),
                  pl.BlockSpec((1, LANES), const),
                  pl.BlockSpec((1, LANES), const),
                  pl.BlockSpec((1, SSD_WIDTH), const),
                  pl.BlockSpec((1, SSD_WIDTH), const),
                  pl.BlockSpec((LANES, SSD_WIDTH), const)],
        out_specs=pl.BlockSpec((SSD_CHUNK, SSD_WIDTH), row_map),
        out_shape=jax.ShapeDtypeStruct((t, SSD_WIDTH), BF16),
        scratch_shapes=[pltpu.VMEM((HALO, CONV_CH), F32),
                        pltpu.VMEM((SSD_GROUPS, SSD_STATE, GROUP_W), F32)],
        compiler_params=_cparams(2),
        name="ssd_mixer",
    )(z, xbc, dt, conv_w.astype(F32), conv_b.astype(F32).reshape(1, CONV_CH), dtb, alog, dfull,
      ssm_norm.astype(F32).reshape(1, SSD_WIDTH), expand)


def _moba_kernel(q_ref, k_ref, v_ref, cos_ref, sin_ref, o_ref, krot_ref, vb_ref, kmean_ref):
    blk = MOBA_BLOCK
    nb = k_ref.shape[0] // blk
    qi = pl.program_id(2)
    scale = MOBA_HEAD_DIM ** -0.5
    half = MOBA_HEAD_DIM // 2

    @pl.when(qi == 0)
    def _():
        kmean_ref[...] = jnp.zeros_like(kmean_ref)
        for j in range(nb):
            rs = slice(j * blk, (j + 1) * blk)
            kj = k_ref[rs, :]
            kr = kj * cos_ref[rs, :] + pltpu.roll(kj, half, axis=1) * sin_ref[rs, :]
            krot_ref[rs, :] = kr.astype(BF16)
            kmean_ref[j:j + 1, :] = jnp.mean(kr, axis=0, keepdims=True)
        vb_ref[...] = v_ref[...].astype(BF16)

    off = pl.multiple_of(qi * blk, blk)
    q = q_ref[...]
    qr = q * cos_ref[pl.ds(off, blk), :] + pltpu.roll(q, half, axis=1) * sin_ref[pl.ds(off, blk), :]
    gate = jnp.dot(qr, kmean_ref[...].T, precision=HIGHEST, preferred_element_type=F32)
    lane = lax.broadcasted_iota(I32, (blk, LANES), 1)
    past = lane < qi
    g = jnp.where(past, gate, -jnp.inf)
    rank = jnp.zeros((blk, LANES), I32)
    for jp in range(nb):
        gj = g[:, jp:jp + 1]
        ahead = jnp.where(gj > g, 1, jnp.where(gj == g, jnp.where(lane > jp, 1, 0), 0))
        rank = rank + ahead
    sel = jnp.where(past, jnp.where(rank < MOBA_TOPK, 1.0, 0.0), 0.0)

    qb = qr.astype(BF16)
    r2 = lax.broadcasted_iota(I32, (blk, blk), 0)
    c2 = lax.broadcasted_iota(I32, (blk, blk), 1)
    s = _dot_nt(qb, krot_ref[pl.ds(off, blk), :]) * scale
    s = jnp.where(c2 <= r2, s, -jnp.inf)
    m = jnp.max(s, axis=1, keepdims=True)
    p = jnp.exp(s - m)
    l = jnp.sum(p, axis=1, keepdims=True)
    acc = jnp.dot(p.astype(BF16), vb_ref[pl.ds(off, blk), :], preferred_element_type=F32)

    def body(j, carry):
        m, l, acc = carry
        o2 = pl.multiple_of(j * blk, blk)
        s = _dot_nt(qb, krot_ref[pl.ds(o2, blk), :]) * scale
        selcol = jnp.sum(jnp.where(lane == j, sel, 0.0), axis=1, keepdims=True)
        s = jnp.where(selcol > 0.0, s, -jnp.inf)
        m_new = jnp.maximum(m, jnp.max(s, axis=1, keepdims=True))
        alpha = jnp.exp(m - m_new)
        p = jnp.exp(s - m_new)
        l = alpha * l + jnp.sum(p, axis=1, keepdims=True)
        acc = alpha * acc + jnp.dot(p.astype(BF16), vb_ref[pl.ds(o2, blk), :], preferred_element_type=F32)
        return m_new, l, acc

    m, l, acc = lax.fori_loop(0, qi, body, (m, l, acc))
    o_ref[...] = (acc / l).astype(o_ref.dtype)


def moba_mixer(qkv, bsz, seq):
    blk = MOBA_BLOCK
    assert seq % blk == 0
    nq = seq // blk
    half = MOBA_HEAD_DIM // 2
    inv = jnp.exp(-math.log(ROPE_THETA) * jnp.arange(half, dtype=F32) / half)
    ang = jnp.arange(seq, dtype=F32)[:, None] * inv[None, :]
    cos2 = jnp.concatenate([jnp.cos(ang), jnp.cos(ang)], axis=1)
    sin2 = jnp.concatenate([-jnp.sin(ang), jnp.sin(ang)], axis=1)
    hh = MOBA_HEADS
    return pl.pallas_call(
        _moba_kernel,
        grid=(bsz, hh, nq),
        in_specs=[pl.BlockSpec((blk, MOBA_HEAD_DIM), lambda b, h, i: (b * nq + i, h)),
                  pl.BlockSpec((seq, MOBA_HEAD_DIM), lambda b, h, i: (b, hh + h)),
                  pl.BlockSpec((seq, MOBA_HEAD_DIM), lambda b, h, i: (b, 2 * hh + h)),
                  pl.BlockSpec((seq, MOBA_HEAD_DIM), lambda b, h, i: (0, 0)),
                  pl.BlockSpec((seq, MOBA_HEAD_DIM), lambda b, h, i: (0, 0))],
        out_specs=pl.BlockSpec((blk, MOBA_HEAD_DIM), lambda b, h, i: (b * nq + i, h)),
        out_shape=jax.ShapeDtypeStruct((bsz * seq, MOBA_WIDTH), BF16),
        scratch_shapes=[pltpu.VMEM((seq, MOBA_HEAD_DIM), BF16),
                        pltpu.VMEM((seq, MOBA_HEAD_DIM), BF16),
                        pltpu.VMEM((LANES, MOBA_HEAD_DIM), F32)],
        compiler_params=_cparams(3),
        name="moba_mixer",
    )(qkv, qkv, qkv, cos2, sin2)


def _gla_kernel(q_ref, k_ref, v_ref, g_ref, gl_ref, wg_ref, bg_ref, hn_ref, o_ref, state_ref):
    L = GLA_CHUNK
    rows = q_ref.shape[0]

    @pl.when(pl.program_id(2) == 0)
    def _():
        state_ref[...] = jnp.zeros_like(state_ref)

    pre = jnp.dot(gl_ref[...], wg_ref[...], precision=HIGHEST, preferred_element_type=F32) + bg_ref[...]
    log_a = -_softplus(-pre) / GLA_GATE_NORM
    r2 = lax.broadcasted_iota(I32, (L, L), 0)
    c2 = lax.broadcasted_iota(I32, (L, L), 1)
    causal = r2 >= c2
    tri = jnp.where(causal, 1.0, 0.0).astype(BF16)
    scale = GLA_DK ** -0.5
    hn = hn_ref[...]
    for c in range(rows // L):
        rs = slice(c * L, (c + 1) * L)
        la = log_a[rs, :]
        gcum = _dot_lhs01(tri, la)
        glast = gcum[L - 1:L, :]
        q = q_ref[rs, :] * scale
        k = k_ref[rs, :]
        vb = v_ref[rs, :].astype(BF16)
        qi = (q * jnp.exp(gcum)).astype(BF16)
        ki = (k * jnp.exp(-gcum)).astype(BF16)
        ke = k * jnp.exp(glast - gcum)
        st = state_ref[...]
        attn = jnp.where(causal, _dot_nt(qi, ki), 0.0)
        o = (jnp.dot(attn.astype(BF16), vb, preferred_element_type=F32)
             + jnp.dot(qi, st.astype(BF16), preferred_element_type=F32))
        decay_col = jnp.exp(jnp.sum(la.T, axis=1, keepdims=True))
        state_ref[...] = st * decay_col + jnp.dot(ke.T.astype(BF16), vb, preferred_element_type=F32)
        o = o * lax.rsqrt(jnp.mean(o * o, axis=-1, keepdims=True) + EPS) * hn
        o_ref[rs, :] = (o * _silu(g_ref[rs, :])).astype(o_ref.dtype)


def gla_core(proj, gl, w_gate2, b_gate, head_norm, bsz, seq):
    t = bsz * seq
    rows = min(GLA_ROWS, seq)
    ns = seq // rows
    wg = jnp.pad(w_gate2.astype(F32), ((0, LANES - GLA_GATE_RANK), (0, 0)))
    kb = GLA_KEY_DIM // GLA_DK
    rmap = lambda off: (lambda b, h, s: (b * ns + s, off + h))
    return pl.pallas_call(
        _gla_kernel,
        grid=(bsz, GLA_HEADS, ns),
        in_specs=[pl.BlockSpec((rows, GLA_DK), rmap(0)),
                  pl.BlockSpec((rows, GLA_DK), rmap(kb)),
                  pl.BlockSpec((rows, GLA_DV), rmap(2 * GLA_KEY_DIM // GLA_DV)),
                  pl.BlockSpec((rows, GLA_DV), rmap((2 * GLA_KEY_DIM + GLA_VAL_DIM) // GLA_DV)),
                  pl.BlockSpec((rows, LANES), lambda b, h, s: (b * ns + s, 0)),
                  pl.BlockSpec((LANES, GLA_DK), lambda b, h, s: (0, h)),
                  pl.BlockSpec((1, GLA_DK), lambda b, h, s: (0, h)),
                  pl.BlockSpec((1, GLA_DV), lambda b, h, s: (0, 0))],
        out_specs=pl.BlockSpec((rows, GLA_DV), rmap(0)),
        out_shape=jax.ShapeDtypeStruct((t, GLA_VAL_DIM), BF16),
        scratch_shapes=[pltpu.VMEM((GLA_DK, GLA_DV), F32)],
        compiler_params=_cparams(3),
        name="gla_core",
    )(proj, proj, proj, proj, gl, wg, b_gate.astype(F32).reshape(1, GLA_KEY_DIM),
      head_norm.astype(F32).reshape(1, GLA_DV))


def _layer_norm_rows(v, g, b):
    mu = jnp.mean(v, axis=-1, keepdims=True)
    d = v - mu
    var = jnp.mean(d * d, axis=-1, keepdims=True)
    return d * lax.rsqrt(var + EPS) * g + b


def _proj_ln_kernel(*refs, n_x, nk_each):
    x_refs = refs[:n_x]
    w_ref, r_ref, g_ref, b_ref, o_ref, ob_ref, acc_ref = refs[n_x:]
    k = pl.program_id(1)

    @pl.when(k == 0)
    def _():
        acc_ref[...] = jnp.zeros_like(acc_ref)

    for n, x_ref in enumerate(x_refs):
        @pl.when(jnp.logical_and(k >= n * nk_each, k < (n + 1) * nk_each))
        def _():
            acc_ref[...] += jnp.dot(x_ref[...], w_ref[...], preferred_element_type=F32)

    @pl.when(k == pl.num_programs(1) - 1)
    def _():
        y = _layer_norm_rows(DN_ALPHA * r_ref[...] + acc_ref[...], g_ref[...], b_ref[...])
        o_ref[...] = y
        ob_ref[...] = y.astype(BF16)


def proj_residual_ln(xs, w, resid, g, b, *, tm=512, tk=512):
    t, kk = xs[0].shape
    d = w.shape[1]
    tm = min(tm, t)
    nk_each = kk // tk
    n_x = len(xs)

    def x_map(n):
        return lambda i, k: (i, jnp.clip(k - n * nk_each, 0, nk_each - 1))

    return pl.pallas_call(
        functools.partial(_proj_ln_kernel, n_x=n_x, nk_each=nk_each),
        grid=(t // tm, n_x * nk_each),
        in_specs=[pl.BlockSpec((tm, tk), x_map(n)) for n in range(n_x)] + [
                  pl.BlockSpec((tk, d), lambda i, k: (k, 0)),
                  pl.BlockSpec((tm, d), lambda i, k: (i, 0)),
                  pl.BlockSpec((1, d), lambda i, k: (0, 0)),
                  pl.BlockSpec((1, d), lambda i, k: (0, 0))],
        out_specs=[pl.BlockSpec((tm, d), lambda i, k: (i, 0)),
                   pl.BlockSpec((tm, d), lambda i, k: (i, 0))],
        out_shape=[jax.ShapeDtypeStruct((t, d), F32), jax.ShapeDtypeStruct((t, d), BF16)],
        scratch_shapes=[pltpu.VMEM((tm, d), F32)],
        compiler_params=_cparams(2),
        name="proj_residual_ln",
    )(*xs, w, resid, g.astype(F32).reshape(1, d), b.astype(F32).reshape(1, d))


def _router_kernel(h_ref, w_ref, b_ref, gate_ref, eidx_ref, rank_ref, cnt_ref, carry_ref):
    tm = h_ref.shape[0]
    i = pl.program_id(0)

    @pl.when(i == 0)
    def _():
        carry_ref[...] = jnp.zeros_like(carry_ref)

    logits = jnp.dot(h_ref[...], w_ref[...], precision=HIGHEST, preferred_element_type=F32) + b_ref[...]
    lane = lax.broadcasted_iota(I32, (tm, LANES), 1)
    work = jnp.where(lane < N_EXPERTS, logits, -jnp.inf)
    vals, idxs = [], []
    multi = jnp.zeros((tm, LANES), F32)
    for _ in range(TOP_K):
        mx = jnp.max(work, axis=1, keepdims=True)
        ix = jnp.min(jnp.where(work == mx, lane, LANES), axis=1, keepdims=True)
        hit = lane == ix
        vals.append(mx)
        idxs.append(ix)
        multi = jnp.where(hit, 1.0, multi)
        work = jnp.where(hit, -jnp.inf, work)
    exps = [jnp.exp(v - vals[0]) for v in vals]
    den = exps[0] + exps[1] + exps[2] + exps[3]
    r2 = lax.broadcasted_iota(I32, (tm, tm), 0)
    c2 = lax.broadcasted_iota(I32, (tm, tm), 1)
    below = jnp.where(r2 > c2, 1.0, 0.0).astype(BF16)
    prior = jnp.dot(below, multi.astype(BF16), preferred_element_type=F32) + carry_ref[...]
    gate = jnp.zeros((tm, LANES), F32)
    eidx = jnp.zeros((tm, LANES), I32)
    rank = jnp.zeros((tm, LANES), I32)
    for k in range(TOP_K):
        rk = jnp.sum(jnp.where(lane == idxs[k], prior, 0.0), axis=1, keepdims=True).astype(I32)
        gate = jnp.where(lane == k, exps[k] / den, gate)
        eidx = jnp.where(lane == k, idxs[k], eidx)
        rank = jnp.where(lane == k, rk, rank)
    gate_ref[...] = gate
    eidx_ref[...] = eidx
    rank_ref[...] = rank
    carry_ref[...] += jnp.sum(multi, axis=0, keepdims=True)
    cnt_ref[...] = carry_ref[...].astype(I32)


def moe_router(h, w_router, b_router, *, tm=512):
    t, d = h.shape
    tm = min(tm, t)
    w = jnp.pad(w_router.astype(F32), ((0, 0), (0, LANES - N_EXPERTS)))
    b = jnp.pad(b_router.astype(F32), (0, LANES - N_EXPERTS)).reshape(1, LANES)
    tile = pl.BlockSpec((tm, LANES), lambda i: (i, 0))
    return pl.pallas_call(
        _router_kernel,
        grid=(t // tm,),
        in_specs=[pl.BlockSpec((tm, d), lambda i: (i, 0)),
                  pl.BlockSpec((d, LANES), lambda i: (0, 0)),
                  pl.BlockSpec((1, LANES), lambda i: (0, 0))],
        out_specs=[tile, tile, tile, pl.BlockSpec((1, LANES), lambda i: (0, 0))],
        out_shape=[jax.ShapeDtypeStruct((t, LANES), F32), jax.ShapeDtypeStruct((t, LANES), I32),
                   jax.ShapeDtypeStruct((t, LANES), I32), jax.ShapeDtypeStruct((1, LANES), I32)],
        scratch_shapes=[pltpu.VMEM((1, LANES), F32)],
        compiler_params=_cparams(1),
        name="moe_router",
    )(h, w, b)


def _dispatch_kernel(slot_ref, pad_lo_ref, nused_ref, h_hbm, xs_hbm, zero_ref, sem, zsem):
    tm = slot_ref.shape[2] // TOP_K
    i = pl.program_id(0)
    zrows = zero_ref.shape[0]
    nblk = xs_hbm.shape[0] // zrows

    @pl.when(i == 0)
    def _():
        zero_ref[...] = jnp.zeros_like(zero_ref)

        def zero_copy(lo):
            return pltpu.make_async_copy(zero_ref, xs_hbm.at[pl.ds(pl.multiple_of(lo, zrows), zrows)], zsem)

        for e in range(N_EXPERTS):
            @pl.when(pad_lo_ref[e] >= 0)
            def _():
                zero_copy(pad_lo_ref[e]).start()
        for e in range(N_EXPERTS):
            @pl.when(pad_lo_ref[e] >= 0)
            def _():
                zero_copy(pad_lo_ref[e]).wait()

        def start_unused(bi, _):
            zero_copy(bi * zrows).start()
            return 0

        def wait_unused(bi, _):
            zero_copy(bi * zrows).wait()
            return 0

        lax.fori_loop(nused_ref[0], nblk, start_unused, 0)
        lax.fori_loop(nused_ref[0], nblk, wait_unused, 0)

    base = i * tm

    def issue(r, _):
        for k in range(TOP_K):
            dst = slot_ref[0, 0, r * TOP_K + k]
            pltpu.make_async_copy(h_hbm.at[pl.ds(base + r, 1)], xs_hbm.at[pl.ds(dst, 1)], sem).start()
        return 0

    lax.fori_loop(0, tm, issue, 0)

    def drain(r, _):
        pltpu.make_async_copy(h_hbm.at[pl.ds(0, 1)], xs_hbm.at[pl.ds(0, 1)], sem).wait()
        return 0

    lax.fori_loop(0, tm * TOP_K, drain, 0)


def moe_dispatch(h, slot, pad_lo, n_used, rows, block, *, tm=512):
    t, d = h.shape
    tm = min(tm, t)
    slot3 = slot.reshape(t // tm, 1, tm * TOP_K)
    return pl.pallas_call(
        _dispatch_kernel,
        grid_spec=pltpu.PrefetchScalarGridSpec(
            num_scalar_prefetch=0,
            grid=(t // tm,),
            in_specs=[pl.BlockSpec((1, 1, tm * TOP_K), lambda i: (i, 0, 0), memory_space=pltpu.SMEM),
                      pl.BlockSpec(memory_space=pltpu.SMEM),
                      pl.BlockSpec(memory_space=pltpu.SMEM),
                      pl.BlockSpec(memory_space=pl.ANY)],
            out_specs=pl.BlockSpec(memory_space=pl.ANY),
            scratch_shapes=[pltpu.VMEM((block, d), h.dtype),
                            pltpu.SemaphoreType.DMA(()),
                            pltpu.SemaphoreType.DMA(())]),
        out_shape=jax.ShapeDtypeStruct((rows, d), h.dtype),
        compiler_params=_cparams(1),
        name="moe_dispatch",
    )(slot3, pad_lo, n_used, h)


def _ffn_kernel(be_ref, nused_ref, x_ref, w1g_ref, w1l_ref, b1g_ref, b1l_ref, w2_ref, b2_ref, o_ref, xb_ref):
    i = pl.program_id(0)
    f = pl.program_id(1)
    used = i < nused_ref[0]

    @pl.when(jnp.logical_and(f == 0, jnp.logical_not(used)))
    def _():
        o_ref[...] = jnp.zeros_like(o_ref)

    @pl.when(jnp.logical_and(f == 0, used))
    def _():
        xb_ref[...] = x_ref[...].astype(BF16)
        o_ref[...] = jnp.broadcast_to(b2_ref[0], o_ref.shape)

    @pl.when(used)
    def _():
        xb = xb_ref[...]
        hg = jnp.dot(xb, w1g_ref[0], preferred_element_type=F32) + b1g_ref[0]
        hl = jnp.dot(xb, w1l_ref[0], preferred_element_type=F32) + b1l_ref[0]
        glu = jnp.minimum(hg, SWIGLU_LIMIT)
        lin = jnp.clip(hl, -SWIGLU_LIMIT, SWIGLU_LIMIT)
        act = glu * jax.nn.sigmoid(SWIGLU_ALPHA * glu) * (lin + 1.0)
        o_ref[...] += jnp.dot(act.astype(BF16), w2_ref[0], preferred_element_type=F32)


def moe_ffn_blocks(xs, block_e, n_used, w1g, w1l, b1g, b1l, w2, b2, *, block, tf=512):
    rows, d = xs.shape
    nblk = rows // block
    ff = w2.shape[1]
    nf = ff // tf

    def fe(i, f, nu):
        return jnp.where(i < nu[0], f, nf - 1)

    return pl.pallas_call(
        _ffn_kernel,
        grid_spec=pltpu.PrefetchScalarGridSpec(
            num_scalar_prefetch=2,
            grid=(nblk, nf),
            in_specs=[pl.BlockSpec((block, d), lambda i, f, be, nu: (jnp.minimum(i, nu[0] - 1), 0)),
                      pl.BlockSpec((1, d, tf), lambda i, f, be, nu: (be[i], 0, fe(i, f, nu))),
                      pl.BlockSpec((1, d, tf), lambda i, f, be, nu: (be[i], 0, fe(i, f, nu))),
                      pl.BlockSpec((1, 1, tf), lambda i, f, be, nu: (be[i], 0, fe(i, f, nu))),
                      pl.BlockSpec((1, 1, tf), lambda i, f, be, nu: (be[i], 0, fe(i, f, nu))),
                      pl.BlockSpec((1, tf, d), lambda i, f, be, nu: (be[i], fe(i, f, nu), 0)),
                      pl.BlockSpec((1, 1, d), lambda i, f, be, nu: (be[i], 0, 0))],
            out_specs=pl.BlockSpec((block, d), lambda i, f, be, nu: (i, 0)),
            scratch_shapes=[pltpu.VMEM((block, d), BF16)]),
        out_shape=jax.ShapeDtypeStruct((rows, d), F32),
        compiler_params=_cparams(2),
        name="moe_ffn",
    )(block_e, n_used, xs, w1g, w1l, b1g, b1l, w2, b2)


def _combine_ln_kernel(slot_ref, ys_hbm, gate_ref, r_ref, g_ref, b_ref, o_ref, ob_ref, buf_ref, sem):
    tc = r_ref.shape[0]

    def issue(r, _):
        for k in range(TOP_K):
            src = slot_ref[0, 0, r * TOP_K + k]
            pltpu.make_async_copy(ys_hbm.at[pl.ds(src, 1)], buf_ref.at[k, pl.ds(r, 1)], sem).start()
        return 0

    lax.fori_loop(0, tc, issue, 0)

    def drain(r, _):
        pltpu.make_async_copy(ys_hbm.at[pl.ds(0, 1)], buf_ref.at[0, pl.ds(0, 1)], sem).wait()
        return 0

    lax.fori_loop(0, tc * TOP_K, drain, 0)

    gate = gate_ref[...]
    ffn = gate[:, 0:1] * buf_ref[0]
    for k in range(1, TOP_K):
        ffn = ffn + gate[:, k:k + 1] * buf_ref[k]
    y = _layer_norm_rows(DN_ALPHA * r_ref[...] + ffn, g_ref[...], b_ref[...])
    o_ref[...] = y
    ob_ref[...] = y.astype(BF16)


def moe_combine_ln(ys, slot, gate, resid, g, b, *, tc=256):
    t, d = resid.shape
    tc = min(tc, t)
    slot3 = slot.reshape(t // tc, 1, tc * TOP_K)
    return pl.pallas_call(
        _combine_ln_kernel,
        grid_spec=pltpu.PrefetchScalarGridSpec(
            num_scalar_prefetch=0,
            grid=(t // tc,),
            in_specs=[pl.BlockSpec((1, 1, tc * TOP_K), lambda i: (i, 0, 0), memory_space=pltpu.SMEM),
                      pl.BlockSpec(memory_space=pl.ANY),
                      pl.BlockSpec((tc, LANES), lambda i: (i, 0)),
                      pl.BlockSpec((tc, d), lambda i: (i, 0)),
                      pl.BlockSpec((1, d), lambda i: (0, 0)),
                      pl.BlockSpec((1, d), lambda i: (0, 0))],
            out_specs=[pl.BlockSpec((tc, d), lambda i: (i, 0)),
                       pl.BlockSpec((tc, d), lambda i: (i, 0))],
            scratch_shapes=[pltpu.VMEM((TOP_K, tc, d), F32),
                            pltpu.SemaphoreType.DMA(())]),
        out_shape=[jax.ShapeDtypeStruct((t, d), F32), jax.ShapeDtypeStruct((t, d), BF16)],
        compiler_params=_cparams(1),
        name="moe_combine_ln",
    )(slot3, ys, gate, resid, g.astype(F32).reshape(1, d), b.astype(F32).reshape(1, d))


MOE_ROWS_BLOCK = 512


def moe_layer(h, w_router, b_router, w1, b1, w2, b2, ln_g, ln_b):
    t, d = h.shape
    block = MOE_ROWS_BLOCK
    gate, eidx, rank, counts = moe_router(h, w_router, b_router)
    cnt = counts[0, :N_EXPERTS]
    padded = (cnt + block - 1) // block * block
    pend = jnp.cumsum(padded)
    pstart = pend - padded
    nblk = -(-(t * TOP_K) // block) + N_EXPERTS
    rows = nblk * block
    n_used = (pend[-1] // block).astype(I32).reshape(1)
    blk_first_row = jnp.minimum(jnp.arange(nblk, dtype=I32), n_used - 1) * block
    block_e = jnp.clip(jnp.searchsorted(pend, blk_first_row, side="right"), 0, N_EXPERTS - 1).astype(I32)
    pad_lo = jnp.where(cnt > 0, pend - block, -1).astype(I32)
    e4 = eidx[:, :TOP_K]
    slot = (jnp.sum(jnp.where(e4[:, :, None] == jnp.arange(N_EXPERTS, dtype=I32)[None, None, :],
                              pstart.astype(I32)[None, None, :], 0), axis=-1)
            + rank[:, :TOP_K]).astype(I32)
    xs = moe_dispatch(h, slot, pad_lo, n_used, rows, block)
    ff = w2.shape[1]
    w1g = w1[:, :, 0::2].astype(BF16)
    w1l = w1[:, :, 1::2].astype(BF16)
    b1g = b1[:, 0::2].astype(F32).reshape(N_EXPERTS, 1, ff)
    b1l = b1[:, 1::2].astype(F32).reshape(N_EXPERTS, 1, ff)
    ys = moe_ffn_blocks(xs, block_e, n_used, w1g, w1l, b1g, b1l, w2.astype(BF16),
                        b2.astype(F32).reshape(N_EXPERTS, 1, d), block=block)
    return moe_combine_ln(ys, slot, gate, h, ln_g, ln_b)


def hybrid_layer(h, hb, bsz, seq, w_in, conv_w, conv_b, dt_bias, a_log, d_skip, ssm_norm, w_out, ln_g, ln_b):
    z = matmul(hb, w_in, n_cols=SSD_WIDTH, col_off=0)
    xbc = matmul(hb, w_in, n_cols=CONV_CH, col_off=SSD_WIDTH)
    dt_off = SSD_WIDTH + CONV_CH
    w_dt = jnp.pad(w_in[:, dt_off:dt_off + SSD_HEADS], ((0, 0), (0, LANES - SSD_HEADS)))
    dt = matmul(hb, w_dt, n_cols=LANES)
    w_qkv = w_in[:, dt_off + SSD_HEADS:]
    qkv = matmul(hb, w_qkv, n_cols=3 * MOBA_WIDTH)
    y_ssd = ssd_mixer(z, xbc, dt, conv_w, conv_b, dt_bias, a_log, d_skip, ssm_norm, bsz, seq)
    y_att = moba_mixer(qkv, bsz, seq)
    return proj_residual_ln([y_ssd, y_att], w_out.astype(BF16), h, ln_g, ln_b)


def gla_layer(h, hb, bsz, seq, w_in, w_gate2, b_gate, head_norm, w_out, ln_g, ln_b):
    n_main = 2 * GLA_KEY_DIM + 2 * GLA_VAL_DIM
    proj = matmul(hb, w_in, n_cols=n_main, col_off=0)
    w_gl = jnp.pad(w_in[:, n_main:n_main + GLA_GATE_RANK], ((0, 0), (0, LANES - GLA_GATE_RANK)))
    gl = matmul(hb, w_gl, n_cols=LANES)
    o = gla_core(proj, gl, w_gate2, b_gate, head_norm, bsz, seq)
    return proj_residual_ln([o], w_out.astype(BF16), h, ln_g, ln_b)


def kernel(x, hyb_w_in, hyb_conv_w, hyb_conv_b, hyb_dt_bias, hyb_a_log, hyb_d, hyb_norm, hyb_w_out, gla_w_in, gla_w_gate2, gla_b_gate, gla_norm, gla_w_out, ln1_g, ln1_b, ln2_g, ln2_b, moe_w_router, moe_b_router, moe_w1, moe_b1, moe_w2, moe_b2):
    bsz, seq, d = x.shape
    h = x.reshape(bsz * seq, d).astype(F32)
    hb = h.astype(BF16)
    for layer in range(DEPTH):
        j = layer // 2
        if layer % 2 == 0:
            h, hb = hybrid_layer(h, hb, bsz, seq, hyb_w_in[j], hyb_conv_w[j], hyb_conv_b[j], hyb_dt_bias[j],
                                 hyb_a_log[j], hyb_d[j], hyb_norm[j], hyb_w_out[j], ln1_g[layer], ln1_b[layer])
        else:
            h, hb = gla_layer(h, hb, bsz, seq, gla_w_in[j], gla_w_gate2[j], gla_b_gate[j], gla_norm[j],
                              gla_w_out[j], ln1_g[layer], ln1_b[layer])
        h, hb = moe_layer(h, moe_w_router[layer], moe_b_router[layer], moe_w1[layer], moe_b1[layer],
                          moe_w2[layer], moe_b2[layer], ln2_g[layer], ln2_b[layer])
    return h.reshape(bsz, seq, d).astype(x.dtype)
```

```python
import functools
import math

import jax
import jax.numpy as jnp
from jax import lax
from jax.experimental import pallas as pl
from jax.experimental.pallas import tpu as pltpu

F32 = jnp.float32
BF16 = jnp.bfloat16
I32 = jnp.int32
HIGHEST = lax.Precision.HIGHEST

D_MODEL = 2048
DEPTH = 2
SSD_HEADS = 32
SSD_HEAD_DIM = 64
SSD_WIDTH = SSD_HEADS * SSD_HEAD_DIM
SSD_GROUPS = 4
SSD_STATE = 128
SSD_CONV = 4
SSD_CHUNK = 128
CONV_CH = SSD_WIDTH + 2 * SSD_GROUPS * SSD_STATE
GROUP_W = SSD_WIDTH // SSD_GROUPS
MOBA_HEADS = 16
MOBA_HEAD_DIM = 128
MOBA_WIDTH = MOBA_HEADS * MOBA_HEAD_DIM
MOBA_BLOCK = 256
MOBA_TOPK = 3
ROPE_THETA = 10000.0
GLA_HEADS = 4
GLA_KEY_DIM = D_MODEL // 2
GLA_VAL_DIM = D_MODEL
GLA_DK = GLA_KEY_DIM // GLA_HEADS
GLA_DV = GLA_VAL_DIM // GLA_HEADS
GLA_GATE_RANK = 16
GLA_GATE_NORM = 16.0
GLA_CHUNK = 64
GLA_ROWS = 512
N_EXPERTS = 32
TOP_K = 4
D_FF = D_MODEL
SWIGLU_LIMIT = 7.0
SWIGLU_ALPHA = 1.702
DN_ALPHA = (2 * DEPTH) ** 0.25
EPS = 1e-5

LANES = 128
HALO = 8
NEG_BIG = -1e30
VMEM_LIMIT = 56 * 1024 * 1024


def _cparams(n_axes, vmem=VMEM_LIMIT):
    return pltpu.CompilerParams(dimension_semantics=("arbitrary",) * n_axes, vmem_limit_bytes=vmem)


def _split3(x):
    hi = x.astype(BF16)
    r = x - hi.astype(F32)
    mid = r.astype(BF16)
    lo = (r - mid.astype(F32)).astype(BF16)
    return lo, mid, hi


def _dot_rhs01(x, e):
    out = None
    for p in _split3(x):
        t = jnp.dot(p, e, preferred_element_type=F32)
        out = t if out is None else out + t
    return out


def _dot_lhs01(t01, x):
    out = None
    for p in _split3(x):
        t = jnp.dot(t01, p, preferred_element_type=F32)
        out = t if out is None else out + t
    return out


def _dot_nt(a, b):
    return lax.dot_general(a, b, (((1,), (1,)), ((), ())), preferred_element_type=F32)


def _silu(x):
    return x * jax.nn.sigmoid(x)


def _softplus(x):
    return jnp.maximum(x, 0.0) + jnp.log1p(jnp.exp(-jnp.abs(x)))


def _mm_kernel(x_ref, w_ref, o_ref, wb_ref):
    @pl.when(pl.program_id(1) == 0)
    def _():
        wb_ref[...] = w_ref[...].astype(BF16)

    o_ref[...] = jnp.dot(x_ref[...], wb_ref[...], preferred_element_type=F32).astype(o_ref.dtype)


def matmul(x, w, *, n_cols, col_off=0, tm=1024, tn=512, out_dtype=F32):
    m, k = x.shape
    tm = min(tm, m)
    tn = min(tn, n_cols)
    assert m % tm == 0 and n_cols % tn == 0 and col_off % tn == 0
    joff = col_off // tn
    return pl.pallas_call(
        _mm_kernel,
        grid=(n_cols // tn, m // tm),
        in_specs=[pl.BlockSpec((tm, k), lambda j, i: (i, 0)),
                  pl.BlockSpec((k, tn), lambda j, i: (0, j + joff))],
        out_specs=pl.BlockSpec((tm, tn), lambda j, i: (i, j)),
        out_shape=jax.ShapeDtypeStruct((m, n_cols), out_dtype),
        scratch_shapes=[pltpu.VMEM((k, tn), BF16)],
        compiler_params=_cparams(2),
        name="dense_matmul",
    )(x, w)


def _ssd_kernel(z_ref, xbc_ref, dt_ref, cw_ref, cb_ref, dtb_ref, alog_ref, dfull_ref, norm_ref, e_ref,
                o_ref, halo_ref, state_ref):
    L = SSD_CHUNK
    c = pl.program_id(1)

    @pl.when(c == 0)
    def _():
        halo_ref[...] = jnp.zeros_like(halo_ref)
        state_ref[...] = jnp.zeros_like(state_ref)

    xbc = xbc_ref[...]
    ext = jnp.concatenate([halo_ref[...], xbc], axis=0)
    cw = cw_ref[...]
    conv = cb_ref[...] + cw[0:1, :] * ext[HALO - 3:HALO - 3 + L, :]
    for j in range(1, SSD_CONV):
        conv = conv + cw[j:j + 1, :] * ext[HALO - 3 + j:HALO - 3 + j + L, :]
    halo_ref[...] = xbc[L - HALO:, :]
    act = _silu(conv)
    xs = act[:, :SSD_WIDTH]

    dt = _softplus(dt_ref[...] + dtb_ref[...])
    adt = -jnp.exp(alog_ref[...]) * dt
    row = lax.broadcasted_iota(I32, (L, L), 0)
    col = lax.broadcasted_iota(I32, (L, L), 1)
    causal = row >= col
    tri = jnp.where(causal, 1.0, 0.0).astype(BF16)
    acum = _dot_lhs01(tri, adt)
    acum_t = acum.T
    e = e_ref[...]
    a_full = _dot_rhs01(acum, e)
    dt_full = _dot_rhs01(dt, e)
    xdt = xs * dt_full
    a_last = a_full[L - 1:L, :]
    exp_a = jnp.exp(a_full)
    chunk_decay = jnp.exp(a_last)
    xdt_b = xdt.astype(BF16)
    xe_b = (xdt * jnp.exp(a_last - a_full)).astype(BF16)
    lane = lax.broadcasted_iota(I32, (L, LANES), 1)

    ys = []
    for g in range(SSD_GROUPS):
        gs = slice(g * GROUP_W, (g + 1) * GROUP_W)
        bg = act[:, SSD_WIDTH + g * SSD_STATE:SSD_WIDTH + (g + 1) * SSD_STATE]
        cg = act[:, SSD_WIDTH + (SSD_GROUPS + g) * SSD_STATE:SSD_WIDTH + (SSD_GROUPS + g + 1) * SSD_STATE]
        bg_b = bg.astype(BF16)
        cg_b = cg.astype(BF16)
        cbm = _dot_nt(cg_b, bg_b)
        st = state_ref[g]
        y_off = jnp.dot(cg_b, st.astype(BF16), preferred_element_type=F32) * exp_a[:, gs]
        state_ref[g] = st * chunk_decay[:, gs] + jnp.dot(bg.T.astype(BF16), xe_b[:, gs],
                                                         preferred_element_type=F32)
        pieces = []
        for pr in range(GROUP_W // LANES):
            xp = xdt_b[:, g * GROUP_W + pr * LANES:g * GROUP_W + (pr + 1) * LANES]
            res = []
            for hh in range(2):
                h = g * (SSD_HEADS // SSD_GROUPS) + pr * 2 + hh
                seg = acum[:, h:h + 1] - acum_t[h:h + 1, :]
                dec = jnp.where(causal, jnp.exp(jnp.where(causal, seg, 0.0)), 0.0)
                res.append(jnp.dot((cbm * dec).astype(BF16), xp, preferred_element_type=F32))
            pieces.append(jnp.where(lane < SSD_HEAD_DIM, res[0], res[1]))
        ys.append(jnp.concatenate(pieces, axis=1) + y_off)
    y = jnp.concatenate(ys, axis=1) + dfull_ref[...] * xs
    y = y * _silu(z_ref[...])
    outs = []
    for g in range(SSD_GROUPS):
        yg = y[:, g * GROUP_W:(g + 1) * GROUP_W]
        outs.append(yg * lax.rsqrt(jnp.mean(yg * yg, axis=-1, keepdims=True) + EPS))
    o_ref[...] = (jnp.concatenate(outs, axis=1) * norm_ref[...]).astype(o_ref.dtype)


def ssd_mixer(z, xbc, dt, conv_w, conv_b, dt_bias, a_log, d_skip, ssm_norm, bsz, seq):
    t = bsz * seq
    nc = seq // SSD_CHUNK
    pad = LANES - SSD_HEADS
    dtb = jnp.pad(dt_bias.astype(F32), (0, pad)).reshape(1, LANES)
    alog = jnp.pad(a_log.astype(F32), (0, pad)).reshape(1, LANES)
    dfull = jnp.repeat(d_skip.astype(F32), SSD_HEAD_DIM).reshape(1, SSD_WIDTH)
    expand = (jnp.arange(LANES)[:, None] == (jnp.arange(SSD_WIDTH) // SSD_HEAD_DIM)[None, :]).astype(BF16)
    row_map = lambda b, c: (b * nc + c, 0)
    const = lambda b, c: (0, 0)
    return pl.pallas_call(
        _ssd_kernel,
        grid=(bsz, nc),
        in_specs=[pl.BlockSpec((SSD_CHUNK, SSD_WIDTH), row_map),
                  pl.BlockSpec((SSD_CHUNK, CONV_CH), row_map),
                  pl.BlockSpec((SSD_CHUNK, LANES), row_map),
                  pl.BlockSpec((SSD_CONV, CONV_CH), const),
                  pl.BlockSpec((1, CONV_CH), const),
                  pl.BlockSpec((1, LANES), const),
                  pl.BlockSpec((1, LANES), const),
                  pl.BlockSpec((1, SSD_WIDTH), const),
                  pl.BlockSpec((1, SSD_WIDTH), const),
                  pl.BlockSpec((LANES, SSD_WIDTH), const)],
        out_specs=pl.BlockSpec((SSD_CHUNK, SSD_WIDTH), row_map),
        out_shape=jax.ShapeDtypeStruct((t, SSD_WIDTH), BF16),
        scratch_shapes=[pltpu.VMEM((HALO, CONV_CH), F32),
                        pltpu.VMEM((SSD_GROUPS, SSD_STATE, GROUP_W), F32)],
        compiler_params=_cparams(2),
        name="ssd_mixer",
    )(z, xbc, dt, conv_w.astype(F32), conv_b.astype(F32).reshape(1, CONV_CH), dtb, alog, dfull,
      ssm_norm.astype(F32).reshape(1, SSD_WIDTH), expand)


def _moba_kernel(q_ref, k_ref, v_ref, cos_ref, sin_ref, o_ref, krot_ref, vt_ref, kmean_ref):
    blk = MOBA_BLOCK
    nb = k_ref.shape[0] // blk
    qi = pl.program_id(2)
    scale = MOBA_HEAD_DIM ** -0.5
    half = MOBA_HEAD_DIM // 2

    @pl.when(qi == 0)
    def _():
        kmean_ref[...] = jnp.zeros_like(kmean_ref)
        for j in range(nb):
            rs = slice(j * blk, (j + 1) * blk)
            kj = k_ref[rs, :]
            kr = kj * cos_ref[rs, :] + pltpu.roll(kj, half, axis=1) * sin_ref[rs, :]
            krot_ref[rs, :] = kr.astype(BF16)
            kmean_ref[j:j + 1, :] = jnp.mean(kr, axis=0, keepdims=True)
            vt_ref[:, rs] = v_ref[rs, :].T.astype(BF16)

    off = pl.multiple_of(qi * blk, blk)
    q = q_ref[...]
    qr = q * cos_ref[pl.ds(off, blk), :] + pltpu.roll(q, half, axis=1) * sin_ref[pl.ds(off, blk), :]
    gate = lax.dot_general(kmean_ref[...], qr, (((1,), (1,)), ((), ())), precision=HIGHEST,
                           preferred_element_type=F32)
    nbp = kmean_ref.shape[0]
    bid = lax.broadcasted_iota(I32, (nbp, blk), 0)
    past = bid < qi
    g = jnp.where(past, gate, -jnp.inf)
    rank = jnp.zeros((nbp, blk), I32)
    for jp in range(nb):
        gj = g[jp:jp + 1, :]
        rank = rank + jnp.where(gj > g, 1, jnp.where(gj == g, jnp.where(bid > jp, 1, 0), 0))
    sel = jnp.where(past, jnp.where(rank < MOBA_TOPK, 1.0, 0.0), 0.0)

    qb = qr.astype(BF16)
    key = lax.broadcasted_iota(I32, (blk, blk), 0)
    qry = lax.broadcasted_iota(I32, (blk, blk), 1)
    s = _dot_nt(krot_ref[pl.ds(off, blk), :], qb) * scale
    s = jnp.where(key <= qry, s, -jnp.inf)
    m = jnp.max(s, axis=0, keepdims=True)
    p = jnp.exp(s - m)
    l = jnp.sum(p, axis=0, keepdims=True)
    acc = jnp.dot(vt_ref[:, pl.ds(off, blk)], p.astype(BF16), preferred_element_type=F32)

    def sel_row(j):
        return jnp.sum(jnp.where(bid == j, sel, 0.0), axis=0, keepdims=True)

    def update(carry, s, vt):
        m, l, acc = carry
        m_new = jnp.maximum(m, jnp.max(s, axis=0, keepdims=True))
        alpha = jnp.exp(m - m_new)
        p = jnp.exp(s - m_new)
        l = alpha * l + jnp.sum(p, axis=0, keepdims=True)
        acc = alpha * acc + jnp.dot(vt, p.astype(BF16), preferred_element_type=F32)
        return m_new, l, acc

    key2 = lax.broadcasted_iota(I32, (2 * blk, blk), 0)

    def pair_body(jj, carry):
        o2 = pl.multiple_of(jj * 2 * blk, 2 * blk)
        s = _dot_nt(krot_ref[pl.ds(o2, 2 * blk), :], qb) * scale
        allowed = jnp.where(key2 < blk, sel_row(2 * jj), sel_row(2 * jj + 1))
        s = jnp.where(allowed > 0.0, s, -jnp.inf)
        return update(carry, s, vt_ref[:, pl.ds(o2, 2 * blk)])

    def single_body(j, carry):
        o2 = pl.multiple_of(j * blk, blk)
        s = _dot_nt(krot_ref[pl.ds(o2, blk), :], qb) * scale
        s = jnp.where(sel_row(j) > 0.0, s, -jnp.inf)
        return update(carry, s, vt_ref[:, pl.ds(o2, blk)])

    npair = qi // 2
    carry = lax.fori_loop(0, npair, pair_body, (m, l, acc))
    m, l, acc = lax.fori_loop(2 * npair, qi, single_body, carry)
    o_ref[...] = (acc / l).T.astype(o_ref.dtype)


def moba_mixer(qkv, bsz, seq):
    blk = MOBA_BLOCK
    assert seq % blk == 0
    nq = seq // blk
    half = MOBA_HEAD_DIM // 2
    inv = jnp.exp(-math.log(ROPE_THETA) * jnp.arange(half, dtype=F32) / half)
    ang = jnp.arange(seq, dtype=F32)[:, None] * inv[None, :]
    cos2 = jnp.concatenate([jnp.cos(ang), jnp.cos(ang)], axis=1)
    sin2 = jnp.concatenate([-jnp.sin(ang), jnp.sin(ang)], axis=1)
    hh = MOBA_HEADS
    return pl.pallas_call(
        _moba_kernel,
        grid=(bsz, hh, nq),
        in_specs=[pl.BlockSpec((blk, MOBA_HEAD_DIM), lambda b, h, i: (b * nq + i, h)),
                  pl.BlockSpec((seq, MOBA_HEAD_DIM), lambda b, h, i: (b, hh + h)),
                  pl.BlockSpec((seq, MOBA_HEAD_DIM), lambda b, h, i: (b, 2 * hh + h)),
                  pl.BlockSpec((seq, MOBA_HEAD_DIM), lambda b, h, i: (0, 0)),
                  pl.BlockSpec((seq, MOBA_HEAD_DIM), lambda b, h, i: (0, 0))],
        out_specs=pl.BlockSpec((blk, MOBA_HEAD_DIM), lambda b, h, i: (b * nq + i, h)),
        out_shape=jax.ShapeDtypeStruct((bsz * seq, MOBA_WIDTH), BF16),
        scratch_shapes=[pltpu.VMEM((seq, MOBA_HEAD_DIM), BF16),
                        pltpu.VMEM((MOBA_HEAD_DIM, seq), BF16),
                        pltpu.VMEM((max(HALO, nq), MOBA_HEAD_DIM), F32)],
        compiler_params=_cparams(3),
        name="moba_mixer",
    )(qkv, qkv, qkv, cos2, sin2)


def _gla_kernel(q_ref, k_ref, v_ref, g_ref, gl_ref, wg_ref, bg_ref, hn_ref, o_ref, state_ref):
    L = GLA_CHUNK
    rows = q_ref.shape[0]

    @pl.when(pl.program_id(2) == 0)
    def _():
        state_ref[...] = jnp.zeros_like(state_ref)

    pre = jnp.dot(gl_ref[...], wg_ref[...], precision=HIGHEST, preferred_element_type=F32) + bg_ref[...]
    log_a = -_softplus(-pre) / GLA_GATE_NORM
    r2 = lax.broadcasted_iota(I32, (L, L), 0)
    c2 = lax.broadcasted_iota(I32, (L, L), 1)
    causal = r2 >= c2
    tri = jnp.where(causal, 1.0, 0.0).astype(BF16)
    scale = GLA_DK ** -0.5
    hn = hn_ref[...]
    for c in range(rows // L):
        rs = slice(c * L, (c + 1) * L)
        la = log_a[rs, :]
        gcum = _dot_lhs01(tri, la)
        glast = gcum[L - 1:L, :]
        q = q_ref[rs, :] * scale
        k = k_ref[rs, :]
        vb = v_ref[rs, :].astype(BF16)
        qi = (q * jnp.exp(gcum)).astype(BF16)
        ki = (k * jnp.exp(-gcum)).astype(BF16)
        ke = k * jnp.exp(glast - gcum)
        st = state_ref[...]
        attn = jnp.where(causal, _dot_nt(qi, ki), 0.0)
        o = (jnp.dot(attn.astype(BF16), vb, preferred_element_type=F32)
             + jnp.dot(qi, st.astype(BF16), preferred_element_type=F32))
        decay_col = jnp.exp(jnp.sum(la.T, axis=1, keepdims=True))
        state_ref[...] = st * decay_col + jnp.dot(ke.T.astype(BF16), vb, preferred_element_type=F32)
        o = o * lax.rsqrt(jnp.mean(o * o, axis=-1, keepdims=True) + EPS) * hn
        o_ref[rs, :] = (o * _silu(g_ref[rs, :])).astype(o_ref.dtype)


def gla_core(proj, gl, w_gate2, b_gate, head_norm, bsz, seq):
    t = bsz * seq
    rows = min(GLA_ROWS, seq)
    ns = seq // rows
    wg = jnp.pad(w_gate2.astype(F32), ((0, LANES - GLA_GATE_RANK), (0, 0)))
    kb = GLA_KEY_DIM // GLA_DK
    rmap = lambda off: (lambda b, h, s: (b * ns + s, off + h))
    return pl.pallas_call(
        _gla_kernel,
        grid=(bsz, GLA_HEADS, ns),
        in_specs=[pl.BlockSpec((rows, GLA_DK), rmap(0)),
                  pl.BlockSpec((rows, GLA_DK), rmap(kb)),
                  pl.BlockSpec((rows, GLA_DV), rmap(2 * GLA_KEY_DIM // GLA_DV)),
                  pl.BlockSpec((rows, GLA_DV), rmap((2 * GLA_KEY_DIM + GLA_VAL_DIM) // GLA_DV)),
                  pl.BlockSpec((rows, LANES), lambda b, h, s: (b * ns + s, 0)),
                  pl.BlockSpec((LANES, GLA_DK), lambda b, h, s: (0, h)),
                  pl.BlockSpec((1, GLA_DK), lambda b, h, s: (0, h)),
                  pl.BlockSpec((1, GLA_DV), lambda b, h, s: (0, 0))],
        out_specs=pl.BlockSpec((rows, GLA_DV), rmap(0)),
        out_shape=jax.ShapeDtypeStruct((t, GLA_VAL_DIM), BF16),
        scratch_shapes=[pltpu.VMEM((GLA_DK, GLA_DV), F32)],
        compiler_params=_cparams(3),
        name="gla_core",
    )(proj, proj, proj, proj, gl, wg, b_gate.astype(F32).reshape(1, GLA_KEY_DIM),
      head_norm.astype(F32).reshape(1, GLA_DV))


def _layer_norm_rows(v, g, b):
    mu = jnp.mean(v, axis=-1, keepdims=True)
    d = v - mu
    var = jnp.mean(d * d, axis=-1, keepdims=True)
    return d * lax.rsqrt(var + EPS) * g + b


def _proj_ln_kernel(*refs, n_x, nk_each):
    x_refs = refs[:n_x]
    w_ref, r_ref, g_ref, b_ref, o_ref, ob_ref, acc_ref = refs[n_x:]
    k = pl.program_id(1)

    @pl.when(k == 0)
    def _():
        acc_ref[...] = jnp.zeros_like(acc_ref)

    for n, x_ref in enumerate(x_refs):
        @pl.when(jnp.logical_and(k >= n * nk_each, k < (n + 1) * nk_each))
        def _():
            acc_ref[...] += jnp.dot(x_ref[...], w_ref[...], preferred_element_type=F32)

    @pl.when(k == pl.num_programs(1) - 1)
    def _():
        y = _layer_norm_rows(DN_ALPHA * r_ref[...] + acc_ref[...], g_ref[...], b_ref[...])
        o_ref[...] = y
        ob_ref[...] = y.astype(BF16)


def proj_residual_ln(xs, w, resid, g, b, *, tm=512, tk=512):
    t, kk = xs[0].shape
    d = w.shape[1]
    tm = min(tm, t)
    nk_each = kk // tk
    n_x = len(xs)

    def x_map(n):
        return lambda i, k: (i, jnp.clip(k - n * nk_each, 0, nk_each - 1))

    return pl.pallas_call(
        functools.partial(_proj_ln_kernel, n_x=n_x, nk_each=nk_each),
        grid=(t // tm, n_x * nk_each),
        in_specs=[pl.BlockSpec((tm, tk), x_map(n)) for n in range(n_x)] + [
                  pl.BlockSpec((tk, d), lambda i, k: (k, 0)),
                  pl.BlockSpec((tm, d), lambda i, k: (i, 0)),
                  pl.BlockSpec((1, d), lambda i, k: (0, 0)),
                  pl.BlockSpec((1, d), lambda i, k: (0, 0))],
        out_specs=[pl.BlockSpec((tm, d), lambda i, k: (i, 0)),
                   pl.BlockSpec((tm, d), lambda i, k: (i, 0))],
        out_shape=[jax.ShapeDtypeStruct((t, d), F32), jax.ShapeDtypeStruct((t, d), BF16)],
        scratch_shapes=[pltpu.VMEM((tm, d), F32)],
        compiler_params=_cparams(2),
        name="proj_residual_ln",
    )(*xs, w, resid, g.astype(F32).reshape(1, d), b.astype(F32).reshape(1, d))


def _router_kernel(h_ref, w_ref, b_ref, gate_ref, eidx_ref, rank_ref, cnt_ref, carry_ref):
    tm = h_ref.shape[0]
    i = pl.program_id(0)

    @pl.when(i == 0)
    def _():
        carry_ref[...] = jnp.zeros_like(carry_ref)

    logits = jnp.dot(h_ref[...], w_ref[...], precision=HIGHEST, preferred_element_type=F32) + b_ref[...]
    lane = lax.broadcasted_iota(I32, (tm, LANES), 1)
    work = jnp.where(lane < N_EXPERTS, logits, -jnp.inf)
    vals, idxs = [], []
    multi = jnp.zeros((tm, LANES), F32)
    for _ in range(TOP_K):
        mx = jnp.max(work, axis=1, keepdims=True)
        ix = jnp.min(jnp.where(work == mx, lane, LANES), axis=1, keepdims=True)
        hit = lane == ix
        vals.append(mx)
        idxs.append(ix)
        multi = jnp.where(hit, 1.0, multi)
        work = jnp.where(hit, -jnp.inf, work)
    exps = [jnp.exp(v - vals[0]) for v in vals]
    den = exps[0] + exps[1] + exps[2] + exps[3]
    r2 = lax.broadcasted_iota(I32, (tm, tm), 0)
    c2 = lax.broadcasted_iota(I32, (tm, tm), 1)
    below = jnp.where(r2 > c2, 1.0, 0.0).astype(BF16)
    prior = jnp.dot(below, multi.astype(BF16), preferred_element_type=F32) + carry_ref[...]
    gate = jnp.zeros((tm, LANES), F32)
    eidx = jnp.zeros((tm, LANES), I32)
    rank = jnp.zeros((tm, LANES), I32)
    for k in range(TOP_K):
        rk = jnp.sum(jnp.where(lane == idxs[k], prior, 0.0), axis=1, keepdims=True).astype(I32)
        gate = jnp.where(lane == k, exps[k] / den, gate)
        eidx = jnp.where(lane == k, idxs[k], eidx)
        rank = jnp.where(lane == k, rk, rank)
    gate_ref[...] = gate
    eidx_ref[...] = eidx
    rank_ref[...] = rank
    carry_ref[...] += jnp.sum(multi, axis=0, keepdims=True)
    cnt_ref[...] = carry_ref[...].astype(I32)


def moe_router(h, w_router, b_router, *, tm=512):
    t, d = h.shape
    tm = min(tm, t)
    w = jnp.pad(w_router.astype(F32), ((0, 0), (0, LANES - N_EXPERTS)))
    b = jnp.pad(b_router.astype(F32), (0, LANES - N_EXPERTS)).reshape(1, LANES)
    tile = pl.BlockSpec((tm, LANES), lambda i: (i, 0))
    return pl.pallas_call(
        _router_kernel,
        grid=(t // tm,),
        in_specs=[pl.BlockSpec((tm, d), lambda i: (i, 0)),
                  pl.BlockSpec((d, LANES), lambda i: (0, 0)),
                  pl.BlockSpec((1, LANES), lambda i: (0, 0))],
        out_specs=[tile, tile, tile, pl.BlockSpec((1, LANES), lambda i: (0, 0))],
        out_shape=[jax.ShapeDtypeStruct((t, LANES), F32), jax.ShapeDtypeStruct((t, LANES), I32),
                   jax.ShapeDtypeStruct((t, LANES), I32), jax.ShapeDtypeStruct((1, LANES), I32)],
        scratch_shapes=[pltpu.VMEM((1, LANES), F32)],
        compiler_params=_cparams(1),
        name="moe_router",
    )(h, w, b)


def _dispatch_kernel(slot_ref, pad_lo_ref, nused_ref, h_ref, xs_hbm, zero_ref, sem, zsem):
    tm = slot_ref.shape[2] // TOP_K
    i = pl.program_id(0)
    zrows = zero_ref.shape[0]
    nblk = xs_hbm.shape[0] // zrows

    @pl.when(i == 0)
    def _():
        zero_ref[...] = jnp.zeros_like(zero_ref)

        def zero_copy(lo):
            return pltpu.make_async_copy(zero_ref, xs_hbm.at[pl.ds(pl.multiple_of(lo, zrows), zrows)], zsem)

        for e in range(N_EXPERTS):
            @pl.when(pad_lo_ref[e] >= 0)
            def _():
                zero_copy(pad_lo_ref[e]).start()
        for e in range(N_EXPERTS):
            @pl.when(pad_lo_ref[e] >= 0)
            def _():
                zero_copy(pad_lo_ref[e]).wait()

        def start_unused(bi, _):
            zero_copy(bi * zrows).start()
            return 0

        def wait_unused(bi, _):
            zero_copy(bi * zrows).wait()
            return 0

        lax.fori_loop(nused_ref[0], nblk, start_unused, 0)
        lax.fori_loop(nused_ref[0], nblk, wait_unused, 0)

    def row_copy(r, dst):
        return pltpu.make_async_copy(h_ref.at[pl.ds(r, 1)], xs_hbm.at[pl.ds(dst, 1)], sem)

    def issue(r, _):
        for k in range(TOP_K):
            row_copy(r, slot_ref[0, 0, r * TOP_K + k]).start()
        return 0

    lax.fori_loop(0, tm, issue, 0)

    def drain(r, _):
        row_copy(0, 0).wait()
        return 0

    lax.fori_loop(0, tm * TOP_K, drain, 0)


def moe_dispatch(h, slot, pad_lo, n_used, rows, block, *, tm=512):
    t, d = h.shape
    tm = min(tm, t)
    slot3 = slot.reshape(t // tm, 1, tm * TOP_K)
    return pl.pallas_call(
        _dispatch_kernel,
        grid_spec=pltpu.PrefetchScalarGridSpec(
            num_scalar_prefetch=0,
            grid=(t // tm,),
            in_specs=[pl.BlockSpec((1, 1, tm * TOP_K), lambda i: (i, 0, 0), memory_space=pltpu.SMEM),
                      pl.BlockSpec(memory_space=pltpu.SMEM),
                      pl.BlockSpec(memory_space=pltpu.SMEM),
                      pl.BlockSpec((tm, d), lambda i: (i, 0))],
            out_specs=pl.BlockSpec(memory_space=pl.ANY),
            scratch_shapes=[pltpu.VMEM((block, d), h.dtype),
                            pltpu.SemaphoreType.DMA(()),
                            pltpu.SemaphoreType.DMA(())]),
        out_shape=jax.ShapeDtypeStruct((rows, d), h.dtype),
        compiler_params=_cparams(1),
        name="moe_dispatch",
    )(slot3, pad_lo, n_used, h)


def _ffn_kernel(be_ref, nused_ref, x_ref, w1_ref, b1_ref, w2_ref, b2_ref, sel_ref, o_ref, xb_ref):
    i = pl.program_id(0)
    f = pl.program_id(1)
    used = i < nused_ref[0]

    @pl.when(jnp.logical_and(f == 0, jnp.logical_not(used)))
    def _():
        o_ref[...] = jnp.zeros_like(o_ref)

    @pl.when(jnp.logical_and(f == 0, used))
    def _():
        xb_ref[...] = x_ref[...].astype(BF16)
        o_ref[...] = jnp.broadcast_to(b2_ref[0], o_ref.shape)

    @pl.when(used)
    def _():
        hb = jnp.dot(xb_ref[...], w1_ref[0].astype(BF16), preferred_element_type=F32) + b1_ref[0]
        width = hb.shape[1]
        nxt = jnp.concatenate([pltpu.roll(hb[:, s:s + LANES], LANES - 1, axis=1)
                               for s in range(0, width, LANES)], axis=1)
        glu = jnp.minimum(hb, SWIGLU_LIMIT)
        lin = jnp.clip(nxt, -SWIGLU_LIMIT, SWIGLU_LIMIT)
        act = glu * jax.nn.sigmoid(SWIGLU_ALPHA * glu) * (lin + 1.0)
        lane = lax.broadcasted_iota(I32, act.shape, 1)
        act = jnp.where(lane % 2 == 0, act, 0.0).astype(BF16)
        act = jnp.dot(act, sel_ref[...], preferred_element_type=F32).astype(BF16)
        o_ref[...] += jnp.dot(act, w2_ref[0].astype(BF16), preferred_element_type=F32)


def moe_ffn_blocks(xs, block_e, n_used, w1, b1, w2, b2, *, block, tf=256):
    rows, d = xs.shape
    nblk = rows // block
    ff = w2.shape[1]
    nf = ff // tf
    sel = (jnp.arange(2 * tf)[:, None] == 2 * jnp.arange(tf)[None, :]).astype(BF16)

    def fe(i, f, nu):
        return jnp.where(i < nu[0], f, nf - 1)

    return pl.pallas_call(
        _ffn_kernel,
        grid_spec=pltpu.PrefetchScalarGridSpec(
            num_scalar_prefetch=2,
            grid=(nblk, nf),
            in_specs=[pl.BlockSpec((block, d), lambda i, f, be, nu: (jnp.minimum(i, nu[0] - 1), 0)),
                      pl.BlockSpec((1, d, 2 * tf), lambda i, f, be, nu: (be[i], 0, fe(i, f, nu))),
                      pl.BlockSpec((1, 1, 2 * tf), lambda i, f, be, nu: (be[i], 0, fe(i, f, nu))),
                      pl.BlockSpec((1, tf, d), lambda i, f, be, nu: (be[i], fe(i, f, nu), 0)),
                      pl.BlockSpec((1, 1, d), lambda i, f, be, nu: (be[i], 0, 0)),
                      pl.BlockSpec((2 * tf, tf), lambda i, f, be, nu: (0, 0))],
            out_specs=pl.BlockSpec((block, d), lambda i, f, be, nu: (i, 0)),
            scratch_shapes=[pltpu.VMEM((block, d), BF16)]),
        out_shape=jax.ShapeDtypeStruct((rows, d), F32),
        compiler_params=_cparams(2),
        name="moe_ffn",
    )(block_e, n_used, xs, w1, b1, w2, b2, sel)


def _combine_ln_kernel(slot_ref, ys_hbm, gate_ref, r_ref, g_ref, b_ref, o_ref, ob_ref, buf_ref, sem):
    tc = r_ref.shape[0]

    def issue(r, _):
        for k in range(TOP_K):
            src = slot_ref[0, 0, r * TOP_K + k]
            pltpu.make_async_copy(ys_hbm.at[pl.ds(src, 1)], buf_ref.at[k, pl.ds(r, 1)], sem).start()
        return 0

    lax.fori_loop(0, tc, issue, 0)

    def drain(r, _):
        pltpu.make_async_copy(ys_hbm.at[pl.ds(0, 1)], buf_ref.at[0, pl.ds(0, 1)], sem).wait()
        return 0

    lax.fori_loop(0, tc * TOP_K, drain, 0)

    gate = gate_ref[...]
    ffn = gate[:, 0:1] * buf_ref[0]
    for k in range(1, TOP_K):
        ffn = ffn + gate[:, k:k + 1] * buf_ref[k]
    y = _layer_norm_rows(DN_ALPHA * r_ref[...] + ffn, g_ref[...], b_ref[...])
    o_ref[...] = y
    ob_ref[...] = y.astype(BF16)


def moe_combine_ln(ys, slot, gate, resid, g, b, *, tc=256):
    t, d = resid.shape
    tc = min(tc, t)
    slot3 = slot.reshape(t // tc, 1, tc * TOP_K)
    return pl.pallas_call(
        _combine_ln_kernel,
        grid_spec=pltpu.PrefetchScalarGridSpec(
            num_scalar_prefetch=0,
            grid=(t // tc,),
            in_specs=[pl.BlockSpec((1, 1, tc * TOP_K), lambda i: (i, 0, 0), memory_space=pltpu.SMEM),
                      pl.BlockSpec(memory_space=pl.ANY),
                      pl.BlockSpec((tc, LANES), lambda i: (i, 0)),
                      pl.BlockSpec((tc, d), lambda i: (i, 0)),
                      pl.BlockSpec((1, d), lambda i: (0, 0)),
                      pl.BlockSpec((1, d), lambda i: (0, 0))],
            out_specs=[pl.BlockSpec((tc, d), lambda i: (i, 0)),
                       pl.BlockSpec((tc, d), lambda i: (i, 0))],
            scratch_shapes=[pltpu.VMEM((TOP_K, tc, d), F32),
                            pltpu.SemaphoreType.DMA(())]),
        out_shape=[jax.ShapeDtypeStruct((t, d), F32), jax.ShapeDtypeStruct((t, d), BF16)],
        compiler_params=_cparams(1),
        name="moe_combine_ln",
    )(slot3, ys, gate, resid, g.astype(F32).reshape(1, d), b.astype(F32).reshape(1, d))


MOE_ROWS_BLOCK = 512


def moe_layer(h, w_router, b_router, w1, b1, w2, b2, ln_g, ln_b):
    t, d = h.shape
    block = MOE_ROWS_BLOCK
    gate, eidx, rank, counts = moe_router(h, w_router, b_router)
    cnt = counts[0, :N_EXPERTS]
    padded = (cnt + block - 1) // block * block
    pend = jnp.cumsum(padded)
    pstart = pend - padded
    nblk = -(-(t * TOP_K) // block) + N_EXPERTS
    rows = nblk * block
    n_used = (pend[-1] // block).astype(I32).reshape(1)
    blk_first_row = jnp.minimum(jnp.arange(nblk, dtype=I32), n_used - 1) * block
    block_e = jnp.minimum(jnp.sum((blk_first_row[:, None] >= pend[None, :]).astype(I32), axis=1),
                          N_EXPERTS - 1).astype(I32)
    pad_lo = jnp.where(cnt > 0, pend - block, -1).astype(I32)
    e4 = eidx[:, :TOP_K]
    slot = (jnp.sum(jnp.where(e4[:, :, None] == jnp.arange(N_EXPERTS, dtype=I32)[None, None, :],
                              pstart.astype(I32)[None, None, :], 0), axis=-1)
            + rank[:, :TOP_K]).astype(I32)
    xs = moe_dispatch(h, slot, pad_lo, n_used, rows, block)
    ys = moe_ffn_blocks(xs, block_e, n_used, w1.astype(F32), b1.astype(F32).reshape(N_EXPERTS, 1, -1),
                        w2.astype(F32), b2.astype(F32).reshape(N_EXPERTS, 1, d), block=block)
    return moe_combine_ln(ys, slot, gate, h, ln_g, ln_b)


def hybrid_layer(h, hb, bsz, seq, w_in, conv_w, conv_b, dt_bias, a_log, d_skip, ssm_norm, w_out, ln_g, ln_b):
    z = matmul(hb, w_in, n_cols=SSD_WIDTH, col_off=0)
    xbc = matmul(hb, w_in, n_cols=CONV_CH, col_off=SSD_WIDTH)
    dt_off = SSD_WIDTH + CONV_CH
    w_dt = jnp.pad(w_in[:, dt_off:dt_off + SSD_HEADS], ((0, 0), (0, LANES - SSD_HEADS)))
    dt = matmul(hb, w_dt, n_cols=LANES)
    w_qkv = w_in[:, dt_off + SSD_HEADS:]
    qkv = matmul(hb, w_qkv, n_cols=3 * MOBA_WIDTH)
    y_ssd = ssd_mixer(z, xbc, dt, conv_w, conv_b, dt_bias, a_log, d_skip, ssm_norm, bsz, seq)
    y_att = moba_mixer(qkv, bsz, seq)
    return proj_residual_ln([y_ssd, y_att], w_out.astype(BF16), h, ln_g, ln_b)


def gla_layer(h, hb, bsz, seq, w_in, w_gate2, b_gate, head_norm, w_out, ln_g, ln_b):
    n_main = 2 * GLA_KEY_DIM + 2 * GLA_VAL_DIM
    proj = matmul(hb, w_in, n_cols=n_main, col_off=0)
    w_gl = jnp.pad(w_in[:, n_main:n_main + GLA_GATE_RANK], ((0, 0), (0, LANES - GLA_GATE_RANK)))
    gl = matmul(hb, w_gl, n_cols=LANES)
    o = gla_core(proj, gl, w_gate2, b_gate, head_norm, bsz, seq)
    return proj_residual_ln([o], w_out.astype(BF16), h, ln_g, ln_b)


def kernel(x, hyb_w_in, hyb_conv_w, hyb_conv_b, hyb_dt_bias, hyb_a_log, hyb_d, hyb_norm, hyb_w_out, gla_w_in, gla_w_gate2, gla_b_gate, gla_norm, gla_w_out, ln1_g, ln1_b, ln2_g, ln2_b, moe_w_router, moe_b_router, moe_w1, moe_b1, moe_w2, moe_b2):
    bsz, seq, d = x.shape
    h = x.reshape(bsz * seq, d).astype(F32)
    hb = h.astype(BF16)
    for layer in range(DEPTH):
        j = layer // 2
        if layer % 2 == 0:
            h, hb = hybrid_layer(h, hb, bsz, seq, hyb_w_in[j], hyb_conv_w[j], hyb_conv_b[j], hyb_dt_bias[j],
                                 hyb_a_log[j], hyb_d[j], hyb_norm[j], hyb_w_out[j], ln1_g[layer], ln1_b[layer])
        else:
            h, hb = gla_layer(h, hb, bsz, seq, gla_w_in[j], gla_w_gate2[j], gla_b_gate[j], gla_norm[j],
                              gla_w_out[j], ln1_g[layer], ln1_b[layer])
        h, hb = moe_layer(h, moe_w_router[layer], moe_b_router[layer], moe_w1[layer], moe_b1[layer],
                          moe_w2[layer], moe_b2[layer], ln2_g[layer], ln2_b[layer])
    return h.reshape(bsz, seq, d).astype(x.dtype)
```

```python
import functools
import math

import jax
import jax.numpy as jnp
from jax import lax
from jax.experimental import pallas as pl
from jax.experimental.pallas import tpu as pltpu

F32 = jnp.float32
BF16 = jnp.bfloat16
I32 = jnp.int32
HIGHEST = lax.Precision.HIGHEST

D_MODEL = 2048
DEPTH = 2
SSD_HEADS = 32
SSD_HEAD_DIM = 64
SSD_WIDTH = SSD_HEADS * SSD_HEAD_DIM
SSD_GROUPS = 4
SSD_STATE = 128
SSD_CONV = 4
SSD_CHUNK = 128
CONV_CH = SSD_WIDTH + 2 * SSD_GROUPS * SSD_STATE
GROUP_W = SSD_WIDTH // SSD_GROUPS
MOBA_HEADS = 16
MOBA_HEAD_DIM = 128
MOBA_WIDTH = MOBA_HEADS * MOBA_HEAD_DIM
MOBA_BLOCK = 256
MOBA_TOPK = 3
ROPE_THETA = 10000.0
GLA_HEADS = 4
GLA_KEY_DIM = D_MODEL // 2
GLA_VAL_DIM = D_MODEL
GLA_DK = GLA_KEY_DIM // GLA_HEADS
GLA_DV = GLA_VAL_DIM // GLA_HEADS
GLA_GATE_RANK = 16
GLA_GATE_NORM = 16.0
GLA_CHUNK = 64
GLA_ROWS = 512
N_EXPERTS = 32
TOP_K = 4
D_FF = D_MODEL
SWIGLU_LIMIT = 7.0
SWIGLU_ALPHA = 1.702
DN_ALPHA = (2 * DEPTH) ** 0.25
EPS = 1e-5

LANES = 128
HALO = 8
NEG_BIG = -1e30
VMEM_LIMIT = 56 * 1024 * 1024


def _cparams(n_axes, vmem=VMEM_LIMIT):
    return pltpu.CompilerParams(dimension_semantics=("arbitrary",) * n_axes, vmem_limit_bytes=vmem)


def _split3(x):
    hi = x.astype(BF16)
    r = x - hi.astype(F32)
    mid = r.astype(BF16)
    lo = (r - mid.astype(F32)).astype(BF16)
    return lo, mid, hi


def _dot_rhs01(x, e):
    out = None
    for p in _split3(x):
        t = jnp.dot(p, e, preferred_element_type=F32)
        out = t if out is None else out + t
    return out


def _dot_lhs01(t01, x):
    out = None
    for p in _split3(x):
        t = jnp.dot(t01, p, preferred_element_type=F32)
        out = t if out is None else out + t
    return out


def _dot_nt(a, b):
    return lax.dot_general(a, b, (((1,), (1,)), ((), ())), preferred_element_type=F32)


def _silu(x):
    return x * jax.nn.sigmoid(x)


def _softplus(x):
    return jnp.maximum(x, 0.0) + jnp.log1p(jnp.exp(-jnp.abs(x)))


def _mm_kernel(*refs, shift):
    if shift:
        x_ref, w_ref, wn_ref, o_ref, wb_ref = refs
    else:
        x_ref, w_ref, o_ref, wb_ref = refs

    @pl.when(pl.program_id(1) == 0)
    def _():
        if shift:
            tn = w_ref.shape[1]
            wcat = jnp.concatenate([w_ref[...], wn_ref[...]], axis=1)
            wb_ref[...] = wcat[:, shift:shift + tn].astype(BF16)
        else:
            wb_ref[...] = w_ref[...].astype(BF16)

    o_ref[...] = jnp.dot(x_ref[...], wb_ref[...], preferred_element_type=F32).astype(o_ref.dtype)


def matmul(x, w, *, n_cols, col_off=0, tm=1024, tn=512, out_dtype=F32):
    m, k = x.shape
    tm = min(tm, m)
    tn = min(tn, n_cols)
    shift = col_off % LANES
    base = col_off - shift
    assert m % tm == 0 and n_cols % tn == 0 and base % tn == 0
    joff = base // tn
    lane_tiles = tn // LANES
    in_specs = [pl.BlockSpec((tm, k), lambda j, i: (i, 0)),
                pl.BlockSpec((k, tn), lambda j, i: (0, j + joff))]
    operands = [x, w]
    if shift:
        in_specs.append(pl.BlockSpec((k, LANES), lambda j, i: (0, (j + joff + 1) * lane_tiles)))
        operands.append(w)
    return pl.pallas_call(
        functools.partial(_mm_kernel, shift=shift),
        grid=(n_cols // tn, m // tm),
        in_specs=in_specs,
        out_specs=pl.BlockSpec((tm, tn), lambda j, i: (i, j)),
        out_shape=jax.ShapeDtypeStruct((m, n_cols), out_dtype),
        scratch_shapes=[pltpu.VMEM((k, tn), BF16)],
        compiler_params=_cparams(2),
        name="dense_matmul",
    )(*operands)


def _ssd_kernel(z_ref, xbc_ref, dt_ref, cw_ref, cb_ref, dtb_ref, alog_ref, dfull_ref, norm_ref, e_ref,
                o_ref, halo_ref, state_ref):
    L = SSD_CHUNK
    c = pl.program_id(1)

    @pl.when(c == 0)
    def _():
        halo_ref[...] = jnp.zeros_like(halo_ref)
        state_ref[...] = jnp.zeros_like(state_ref)

    xbc = xbc_ref[...]
    ext = jnp.concatenate([halo_ref[...], xbc], axis=0)
    cw = cw_ref[...]
    conv = cb_ref[...] + cw[0:1, :] * ext[HALO - 3:HALO - 3 + L, :]
    for j in range(1, SSD_CONV):
        conv = conv + cw[j:j + 1, :] * ext[HALO - 3 + j:HALO - 3 + j + L, :]
    halo_ref[...] = xbc[L - HALO:, :]
    act = _silu(conv)
    xs = act[:, :SSD_WIDTH]

    dt = _softplus(dt_ref[...] + dtb_ref[...])
    adt = -jnp.exp(alog_ref[...]) * dt
    row = lax.broadcasted_iota(I32, (L, L), 0)
    col = lax.broadcasted_iota(I32, (L, L), 1)
    causal = row >= col
    tri = jnp.where(causal, 1.0, 0.0).astype(BF16)
    acum = _dot_lhs01(tri, adt)
    acum_t = acum.T
    e = e_ref[...]
    a_full = _dot_rhs01(acum, e)
    dt_full = _dot_rhs01(dt, e)
    xdt = xs * dt_full
    a_last = a_full[L - 1:L, :]
    exp_a = jnp.exp(a_full)
    chunk_decay = jnp.exp(a_last)
    xdt_b = xdt.astype(BF16)
    xe_b = (xdt * jnp.exp(a_last - a_full)).astype(BF16)
    lane = lax.broadcasted_iota(I32, (L, LANES), 1)

    ys = []
    for g in range(SSD_GROUPS):
        gs = slice(g * GROUP_W, (g + 1) * GROUP_W)
        bg = act[:, SSD_WIDTH + g * SSD_STATE:SSD_WIDTH + (g + 1) * SSD_STATE]
        cg = act[:, SSD_WIDTH + (SSD_GROUPS + g) * SSD_STATE:SSD_WIDTH + (SSD_GROUPS + g + 1) * SSD_STATE]
        bg_b = bg.astype(BF16)
        cg_b = cg.astype(BF16)
        cbm = _dot_nt(cg_b, bg_b)
        st = state_ref[g]
        y_off = jnp.dot(cg_b, st.astype(BF16), preferred_element_type=F32) * exp_a[:, gs]
        state_ref[g] = st * chunk_decay[:, gs] + jnp.dot(bg.T.astype(BF16), xe_b[:, gs],
                                                         preferred_element_type=F32)
        pieces = []
        for pr in range(GROUP_W // LANES):
            xp = xdt_b[:, g * GROUP_W + pr * LANES:g * GROUP_W + (pr + 1) * LANES]
            res = []
            for hh in range(2):
                h = g * (SSD_HEADS // SSD_GROUPS) + pr * 2 + hh
                seg = acum[:, h:h + 1] - acum_t[h:h + 1, :]
                dec = jnp.where(causal, jnp.exp(jnp.where(causal, seg, 0.0)), 0.0)
                res.append(jnp.dot((cbm * dec).astype(BF16), xp, preferred_element_type=F32))
            pieces.append(jnp.where(lane < SSD_HEAD_DIM, res[0], res[1]))
        ys.append(jnp.concatenate(pieces, axis=1) + y_off)
    y = jnp.concatenate(ys, axis=1) + dfull_ref[...] * xs
    y = y * _silu(z_ref[...])
    outs = []
    for g in range(SSD_GROUPS):
        yg = y[:, g * GROUP_W:(g + 1) * GROUP_W]
        outs.append(yg * lax.rsqrt(jnp.mean(yg * yg, axis=-1, keepdims=True) + EPS))
    o_ref[...] = (jnp.concatenate(outs, axis=1) * norm_ref[...]).astype(o_ref.dtype)


def ssd_mixer(z, xbc, dt, conv_w, conv_b, dt_bias, a_log, d_skip, ssm_norm, bsz, seq):
    t = bsz * seq
    nc = seq // SSD_CHUNK
    pad = LANES - SSD_HEADS
    dtb = jnp.pad(dt_bias.astype(F32), (0, pad)).reshape(1, LANES)
    alog = jnp.pad(a_log.astype(F32), (0, pad)).reshape(1, LANES)
    dfull = jnp.repeat(d_skip.astype(F32), SSD_HEAD_DIM).reshape(1, SSD_WIDTH)
    expand = (jnp.arange(LANES)[:, None] == (jnp.arange(SSD_WIDTH) // SSD_HEAD_DIM)[None, :]).astype(BF16)
    row_map = lambda b, c: (b * nc + c, 0)
    const = lambda b, c: (0, 0)
    return pl.pallas_call(
        _ssd_kernel,
        grid=(bsz, nc),
        in_specs=[pl.BlockSpec((SSD_CHUNK, SSD_WIDTH), row_map),
                  pl.BlockSpec((SSD_CHUNK, CONV_CH), row_map),
                  pl.BlockSpec((SSD_CHUNK, LANES), row_map),
                  pl.BlockSpec((SSD_CONV, CONV_CH), const),
                  pl.BlockSpec((1, CONV_CH), const),
                  pl.BlockSpec((1, LANES), const),
                  pl.BlockSpec((1, LANES), const),
                  pl.BlockSpec((1, SSD_WIDTH), const),
                  pl.BlockSpec((1, SSD_WIDTH), const),
                  pl.BlockSpec((LANES, SSD_WIDTH), const)],
        out_specs=pl.BlockSpec((SSD_CHUNK, SSD_WIDTH), row_map),
        out_shape=jax.ShapeDtypeStruct((t, SSD_WIDTH), BF16),
        scratch_shapes=[pltpu.VMEM((HALO, CONV_CH), F32),
                        pltpu.VMEM((SSD_GROUPS, SSD_STATE, GROUP_W), F32)],
        compiler_params=_cparams(2),
        name="ssd_mixer",
    )(z, xbc, dt, conv_w.astype(F32), conv_b.astype(F32).reshape(1, CONV_CH), dtb, alog, dfull,
      ssm_norm.astype(F32).reshape(1, SSD_WIDTH), expand)


def _moba_kernel(q_ref, k_ref, v_ref, cos_ref, sin_ref, o_ref, krot_ref, vt_ref, kmean_ref):
    blk = MOBA_BLOCK
    nb = k_ref.shape[0] // blk
    qi = pl.program_id(2)
    scale = MOBA_HEAD_DIM ** -0.5
    half = MOBA_HEAD_DIM // 2

    @pl.when(qi == 0)
    def _():
        kmean_ref[...] = jnp.zeros_like(kmean_ref)
        for j in range(nb):
            rs = slice(j * blk, (j + 1) * blk)
            kj = k_ref[rs, :]
            kr = kj * cos_ref[rs, :] + pltpu.roll(kj, half, axis=1) * sin_ref[rs, :]
            krot_ref[rs, :] = kr.astype(BF16)
            kmean_ref[j:j + 1, :] = jnp.mean(kr, axis=0, keepdims=True)
            vt_ref[:, rs] = v_ref[rs, :].T.astype(BF16)

    off = pl.multiple_of(qi * blk, blk)
    q = q_ref[...]
    qr = q * cos_ref[pl.ds(off, blk), :] + pltpu.roll(q, half, axis=1) * sin_ref[pl.ds(off, blk), :]
    gate = lax.dot_general(kmean_ref[...], qr, (((1,), (1,)), ((), ())), precision=HIGHEST,
                           preferred_element_type=F32)
    nbp = kmean_ref.shape[0]
    bid = lax.broadcasted_iota(I32, (nbp, blk), 0)
    past = bid < qi
    g = jnp.where(past, gate, -jnp.inf)
    rank = jnp.zeros((nbp, blk), I32)
    for jp in range(nb):
        gj = g[jp:jp + 1, :]
        rank = rank + jnp.where(gj > g, 1, jnp.where(gj == g, jnp.where(bid > jp, 1, 0), 0))
    sel = jnp.where(past, jnp.where(rank < MOBA_TOPK, 1.0, 0.0), 0.0)

    qb = qr.astype(BF16)
    key = lax.broadcasted_iota(I32, (blk, blk), 0)
    qry = lax.broadcasted_iota(I32, (blk, blk), 1)
    s = _dot_nt(krot_ref[pl.ds(off, blk), :], qb) * scale
    s = jnp.where(key <= qry, s, -jnp.inf)
    m = jnp.max(s, axis=0, keepdims=True)
    p = jnp.exp(s - m)
    l = jnp.sum(p, axis=0, keepdims=True)
    acc = jnp.dot(vt_ref[:, pl.ds(off, blk)], p.astype(BF16), preferred_element_type=F32)

    def sel_row(j):
        return jnp.sum(jnp.where(bid == j, sel, 0.0), axis=0, keepdims=True)

    def update(carry, s, vt):
        m, l, acc = carry
        m_new = jnp.maximum(m, jnp.max(s, axis=0, keepdims=True))
        alpha = jnp.exp(m - m_new)
        p = jnp.exp(s - m_new)
        l = alpha * l + jnp.sum(p, axis=0, keepdims=True)
        acc = alpha * acc + jnp.dot(vt, p.astype(BF16), preferred_element_type=F32)
        return m_new, l, acc

    key2 = lax.broadcasted_iota(I32, (2 * blk, blk), 0)

    def pair_body(jj, carry):
        o2 = pl.multiple_of(jj * 2 * blk, 2 * blk)
        s = _dot_nt(krot_ref[pl.ds(o2, 2 * blk), :], qb) * scale
        allowed = jnp.where(key2 < blk, sel_row(2 * jj), sel_row(2 * jj + 1))
        s = jnp.where(allowed > 0.0, s, -jnp.inf)
        return update(carry, s, vt_ref[:, pl.ds(o2, 2 * blk)])

    def single_body(j, carry):
        o2 = pl.multiple_of(j * blk, blk)
        s = _dot_nt(krot_ref[pl.ds(o2, blk), :], qb) * scale
        s = jnp.where(sel_row(j) > 0.0, s, -jnp.inf)
        return update(carry, s, vt_ref[:, pl.ds(o2, blk)])

    npair = qi // 2
    carry = lax.fori_loop(0, npair, pair_body, (m, l, acc))
    m, l, acc = lax.fori_loop(2 * npair, qi, single_body, carry)
    o_ref[...] = (acc / l).T.astype(o_ref.dtype)


def moba_mixer(qkv, bsz, seq):
    blk = MOBA_BLOCK
    assert seq % blk == 0
    nq = seq // blk
    half = MOBA_HEAD_DIM // 2
    inv = jnp.exp(-math.log(ROPE_THETA) * jnp.arange(half, dtype=F32) / half)
    ang = jnp.arange(seq, dtype=F32)[:, None] * inv[None, :]
    cos2 = jnp.concatenate([jnp.cos(ang), jnp.cos(ang)], axis=1)
    sin2 = jnp.concatenate([-jnp.sin(ang), jnp.sin(ang)], axis=1)
    hh = MOBA_HEADS
    return pl.pallas_call(
        _moba_kernel,
        grid=(bsz, hh, nq),
        in_specs=[pl.BlockSpec((blk, MOBA_HEAD_DIM), lambda b, h, i: (b * nq + i, h)),
                  pl.BlockSpec((seq, MOBA_HEAD_DIM), lambda b, h, i: (b, hh + h)),
                  pl.BlockSpec((seq, MOBA_HEAD_DIM), lambda b, h, i: (b, 2 * hh + h)),
                  pl.BlockSpec((seq, MOBA_HEAD_DIM), lambda b, h, i: (0, 0)),
                  pl.BlockSpec((seq, MOBA_HEAD_DIM), lambda b, h, i: (0, 0))],
        out_specs=pl.BlockSpec((blk, MOBA_HEAD_DIM), lambda b, h, i: (b * nq + i, h)),
        out_shape=jax.ShapeDtypeStruct((bsz * seq, MOBA_WIDTH), BF16),
        scratch_shapes=[pltpu.VMEM((seq, MOBA_HEAD_DIM), BF16),
                        pltpu.VMEM((MOBA_HEAD_DIM, seq), BF16),
                        pltpu.VMEM((max(HALO, nq), MOBA_HEAD_DIM), F32)],
        compiler_params=_cparams(3),
        name="moba_mixer",
    )(qkv, qkv, qkv, cos2, sin2)


def _gla_kernel(q_ref, k_ref, v_ref, g_ref, gl_ref, wg_ref, bg_ref, hn_ref, o_ref, state_ref):
    L = GLA_CHUNK
    rows = q_ref.shape[0]

    @pl.when(pl.program_id(2) == 0)
    def _():
        state_ref[...] = jnp.zeros_like(state_ref)

    gl = gl_ref[...]
    gl = jnp.where(lax.broadcasted_iota(I32, gl.shape, 1) < GLA_GATE_RANK, gl, 0.0)
    pre = jnp.dot(gl, wg_ref[...], precision=HIGHEST, preferred_element_type=F32) + bg_ref[...]
    log_a = -_softplus(-pre) / GLA_GATE_NORM
    r2 = lax.broadcasted_iota(I32, (L, L), 0)
    c2 = lax.broadcasted_iota(I32, (L, L), 1)
    causal = r2 >= c2
    tri = jnp.where(causal, 1.0, 0.0).astype(BF16)
    scale = GLA_DK ** -0.5
    hn = hn_ref[...]
    for c in range(rows // L):
        rs = slice(c * L, (c + 1) * L)
        la = log_a[rs, :]
        gcum = _dot_lhs01(tri, la)
        glast = gcum[L - 1:L, :]
        q = q_ref[rs, :] * scale
        k = k_ref[rs, :]
        vb = v_ref[rs, :].astype(BF16)
        qi = (q * jnp.exp(gcum)).astype(BF16)
        ki = (k * jnp.exp(-gcum)).astype(BF16)
        ke = k * jnp.exp(glast - gcum)
        st = state_ref[...]
        attn = jnp.where(causal, _dot_nt(qi, ki), 0.0)
        o = (jnp.dot(attn.astype(BF16), vb, preferred_element_type=F32)
             + jnp.dot(qi, st.astype(BF16), preferred_element_type=F32))
        decay_col = jnp.exp(jnp.sum(la.T, axis=1, keepdims=True))
        state_ref[...] = st * decay_col + jnp.dot(ke.T.astype(BF16), vb, preferred_element_type=F32)
        o = o * lax.rsqrt(jnp.mean(o * o, axis=-1, keepdims=True) + EPS) * hn
        o_ref[rs, :] = (o * _silu(g_ref[rs, :])).astype(o_ref.dtype)


def gla_core(proj, gl, w_gate2, b_gate, head_norm, bsz, seq):
    t = bsz * seq
    rows = min(GLA_ROWS, seq)
    ns = seq // rows
    wg = jnp.pad(w_gate2.astype(F32), ((0, LANES - GLA_GATE_RANK), (0, 0)))
    kb = GLA_KEY_DIM // GLA_DK
    rmap = lambda off: (lambda b, h, s: (b * ns + s, off + h))
    return pl.pallas_call(
        _gla_kernel,
        grid=(bsz, GLA_HEADS, ns),
        in_specs=[pl.BlockSpec((rows, GLA_DK), rmap(0)),
                  pl.BlockSpec((rows, GLA_DK), rmap(kb)),
                  pl.BlockSpec((rows, GLA_DV), rmap(2 * GLA_KEY_DIM // GLA_DV)),
                  pl.BlockSpec((rows, GLA_DV), rmap((2 * GLA_KEY_DIM + GLA_VAL_DIM) // GLA_DV)),
                  pl.BlockSpec((rows, LANES), lambda b, h, s: (b * ns + s, 0)),
                  pl.BlockSpec((LANES, GLA_DK), lambda b, h, s: (0, h)),
                  pl.BlockSpec((1, GLA_DK), lambda b, h, s: (0, h)),
                  pl.BlockSpec((1, GLA_DV), lambda b, h, s: (0, 0))],
        out_specs=pl.BlockSpec((rows, GLA_DV), rmap(0)),
        out_shape=jax.ShapeDtypeStruct((t, GLA_VAL_DIM), BF16),
        scratch_shapes=[pltpu.VMEM((GLA_DK, GLA_DV), F32)],
        compiler_params=_cparams(3),
        name="gla_core",
    )(proj, proj, proj, proj, gl, wg, b_gate.astype(F32).reshape(1, GLA_KEY_DIM),
      head_norm.astype(F32).reshape(1, GLA_DV))


def _layer_norm_rows(v, g, b):
    mu = jnp.mean(v, axis=-1, keepdims=True)
    d = v - mu
    var = jnp.mean(d * d, axis=-1, keepdims=True)
    return d * lax.rsqrt(var + EPS) * g + b


def _proj_ln_kernel(*refs, n_x, nk_each):
    x_refs = refs[:n_x]
    w_ref, r_ref, g_ref, b_ref, o_ref, ob_ref, acc_ref = refs[n_x:]
    k = pl.program_id(1)

    @pl.when(k == 0)
    def _():
        acc_ref[...] = jnp.zeros_like(acc_ref)

    for n, x_ref in enumerate(x_refs):
        @pl.when(jnp.logical_and(k >= n * nk_each, k < (n + 1) * nk_each))
        def _():
            acc_ref[...] += jnp.dot(x_ref[...], w_ref[...], preferred_element_type=F32)

    @pl.when(k == pl.num_programs(1) - 1)
    def _():
        y = _layer_norm_rows(DN_ALPHA * r_ref[...] + acc_ref[...], g_ref[...], b_ref[...])
        o_ref[...] = y
        ob_ref[...] = y.astype(BF16)


def proj_residual_ln(xs, w, resid, g, b, *, tm=512, tk=512):
    t, kk = xs[0].shape
    d = w.shape[1]
    tm = min(tm, t)
    nk_each = kk // tk
    n_x = len(xs)

    def x_map(n):
        return lambda i, k: (i, jnp.clip(k - n * nk_each, 0, nk_each - 1))

    return pl.pallas_call(
        functools.partial(_proj_ln_kernel, n_x=n_x, nk_each=nk_each),
        grid=(t // tm, n_x * nk_each),
        in_specs=[pl.BlockSpec((tm, tk), x_map(n)) for n in range(n_x)] + [
                  pl.BlockSpec((tk, d), lambda i, k: (k, 0)),
                  pl.BlockSpec((tm, d), lambda i, k: (i, 0)),
                  pl.BlockSpec((1, d), lambda i, k: (0, 0)),
                  pl.BlockSpec((1, d), lambda i, k: (0, 0))],
        out_specs=[pl.BlockSpec((tm, d), lambda i, k: (i, 0)),
                   pl.BlockSpec((tm, d), lambda i, k: (i, 0))],
        out_shape=[jax.ShapeDtypeStruct((t, d), F32), jax.ShapeDtypeStruct((t, d), BF16)],
        scratch_shapes=[pltpu.VMEM((tm, d), F32)],
        compiler_params=_cparams(2),
        name="proj_residual_ln",
    )(*xs, w, resid, g.astype(F32).reshape(1, d), b.astype(F32).reshape(1, d))


def _router_kernel(h_ref, w_ref, b_ref, gate_ref, eidx_ref, rank_ref, cnt_ref, carry_ref):
    tm = h_ref.shape[0]
    i = pl.program_id(0)

    @pl.when(i == 0)
    def _():
        carry_ref[...] = jnp.zeros_like(carry_ref)

    logits = jnp.dot(h_ref[...], w_ref[...], precision=HIGHEST, preferred_element_type=F32) + b_ref[...]
    lane = lax.broadcasted_iota(I32, (tm, LANES), 1)
    work = jnp.where(lane < N_EXPERTS, logits, -jnp.inf)
    vals, idxs = [], []
    multi = jnp.zeros((tm, LANES), F32)
    for _ in range(TOP_K):
        mx = jnp.max(work, axis=1, keepdims=True)
        ix = jnp.min(jnp.where(work == mx, lane, LANES), axis=1, keepdims=True)
        hit = lane == ix
        vals.append(mx)
        idxs.append(ix)
        multi = jnp.where(hit, 1.0, multi)
        work = jnp.where(hit, -jnp.inf, work)
    exps = [jnp.exp(v - vals[0]) for v in vals]
    den = exps[0] + exps[1] + exps[2] + exps[3]
    r2 = lax.broadcasted_iota(I32, (tm, tm), 0)
    c2 = lax.broadcasted_iota(I32, (tm, tm), 1)
    below = jnp.where(r2 > c2, 1.0, 0.0).astype(BF16)
    prior = jnp.dot(below, multi.astype(BF16), preferred_element_type=F32) + carry_ref[...]
    gate = jnp.zeros((tm, LANES), F32)
    eidx = jnp.zeros((tm, LANES), I32)
    rank = jnp.zeros((tm, LANES), I32)
    for k in range(TOP_K):
        rk = jnp.sum(jnp.where(lane == idxs[k], prior, 0.0), axis=1, keepdims=True).astype(I32)
        gate = jnp.where(lane == k, exps[k] / den, gate)
        eidx = jnp.where(lane == k, idxs[k], eidx)
        rank = jnp.where(lane == k, rk, rank)
    gate_ref[...] = gate
    eidx_ref[...] = eidx
    rank_ref[...] = rank
    carry_ref[...] += jnp.sum(multi, axis=0, keepdims=True)
    cnt_ref[...] = carry_ref[...].astype(I32)


def moe_router(h, w_router, b_router, *, tm=512):
    t, d = h.shape
    tm = min(tm, t)
    w = jnp.pad(w_router.astype(F32), ((0, 0), (0, LANES - N_EXPERTS)))
    b = jnp.pad(b_router.astype(F32), (0, LANES - N_EXPERTS)).reshape(1, LANES)
    tile = pl.BlockSpec((tm, LANES), lambda i: (i, 0))
    return pl.pallas_call(
        _router_kernel,
        grid=(t // tm,),
        in_specs=[pl.BlockSpec((tm, d), lambda i: (i, 0)),
                  pl.BlockSpec((d, LANES), lambda i: (0, 0)),
                  pl.BlockSpec((1, LANES), lambda i: (0, 0))],
        out_specs=[tile, tile, tile, pl.BlockSpec((1, LANES), lambda i: (0, 0))],
        out_shape=[jax.ShapeDtypeStruct((t, LANES), F32), jax.ShapeDtypeStruct((t, LANES), I32),
                   jax.ShapeDtypeStruct((t, LANES), I32), jax.ShapeDtypeStruct((1, LANES), I32)],
        scratch_shapes=[pltpu.VMEM((1, LANES), F32)],
        compiler_params=_cparams(1),
        name="moe_router",
    )(h, w, b)


def _dispatch_kernel(slot_ref, pad_lo_ref, nused_ref, h_ref, xs_hbm, zero_ref, sem, zsem):
    tm = slot_ref.shape[2] // TOP_K
    i = pl.program_id(0)
    zrows = zero_ref.shape[0]
    nblk = xs_hbm.shape[0] // zrows

    @pl.when(i == 0)
    def _():
        zero_ref[...] = jnp.zeros_like(zero_ref)

        def zero_copy(lo):
            return pltpu.make_async_copy(zero_ref, xs_hbm.at[pl.ds(pl.multiple_of(lo, zrows), zrows)], zsem)

        for e in range(N_EXPERTS):
            @pl.when(pad_lo_ref[e] >= 0)
            def _():
                zero_copy(pad_lo_ref[e]).start()
        for e in range(N_EXPERTS):
            @pl.when(pad_lo_ref[e] >= 0)
            def _():
                zero_copy(pad_lo_ref[e]).wait()

        def start_unused(bi, _):
            zero_copy(bi * zrows).start()
            return 0

        def wait_unused(bi, _):
            zero_copy(bi * zrows).wait()
            return 0

        lax.fori_loop(nused_ref[0], nblk, start_unused, 0)
        lax.fori_loop(nused_ref[0], nblk, wait_unused, 0)

    def row_copy(r, dst):
        return pltpu.make_async_copy(h_ref.at[pl.ds(r, 1)], xs_hbm.at[pl.ds(dst, 1)], sem)

    def issue(r, _):
        for k in range(TOP_K):
            row_copy(r, slot_ref[0, 0, r * TOP_K + k]).start()
        return 0

    lax.fori_loop(0, tm, issue, 0)

    def drain(r, _):
        row_copy(0, 0).wait()
        return 0

    lax.fori_loop(0, tm * TOP_K, drain, 0, unroll=16)


def moe_dispatch(h, slot, pad_lo, n_used, rows, block, *, tm=512):
    t, d = h.shape
    tm = min(tm, t)
    slot3 = slot.reshape(t // tm, 1, tm * TOP_K)
    return pl.pallas_call(
        _dispatch_kernel,
        grid_spec=pltpu.PrefetchScalarGridSpec(
            num_scalar_prefetch=0,
            grid=(t // tm,),
            in_specs=[pl.BlockSpec((1, 1, tm * TOP_K), lambda i: (i, 0, 0), memory_space=pltpu.SMEM),
                      pl.BlockSpec(memory_space=pltpu.SMEM),
                      pl.BlockSpec(memory_space=pltpu.SMEM),
                      pl.BlockSpec((tm, d), lambda i: (i, 0))],
            out_specs=pl.BlockSpec(memory_space=pl.ANY),
            scratch_shapes=[pltpu.VMEM((block, d), h.dtype),
                            pltpu.SemaphoreType.DMA(()),
                            pltpu.SemaphoreType.DMA(())]),
        out_shape=jax.ShapeDtypeStruct((rows, d), h.dtype),
        compiler_params=_cparams(1),
        name="moe_dispatch",
    )(slot3, pad_lo, n_used, h)


def _ffn_kernel(be_ref, nused_ref, x_ref, w1_ref, b1_ref, w2_ref, b2_ref, sel_ref, o_ref):
    i = pl.program_id(0)
    f = pl.program_id(1)
    used = i < nused_ref[0]

    @pl.when(jnp.logical_and(f == 0, jnp.logical_not(used)))
    def _():
        o_ref[...] = jnp.zeros_like(o_ref)

    @pl.when(jnp.logical_and(f == 0, used))
    def _():
        o_ref[...] = jnp.broadcast_to(b2_ref[0], o_ref.shape)

    @pl.when(used)
    def _():
        hb = jnp.dot(x_ref[...].astype(BF16), w1_ref[0].astype(BF16), preferred_element_type=F32) + b1_ref[0]
        width = hb.shape[1]
        nxt = jnp.concatenate([pltpu.roll(hb[:, s:s + LANES], LANES - 1, axis=1)
                               for s in range(0, width, LANES)], axis=1)
        glu = jnp.minimum(hb, SWIGLU_LIMIT)
        lin = jnp.clip(nxt, -SWIGLU_LIMIT, SWIGLU_LIMIT)
        act = glu * jax.nn.sigmoid(SWIGLU_ALPHA * glu) * (lin + 1.0)
        lane = lax.broadcasted_iota(I32, act.shape, 1)
        act = jnp.where(lane % 2 == 0, act, 0.0).astype(BF16)
        act = jnp.dot(act, sel_ref[...], preferred_element_type=F32).astype(BF16)
        o_ref[...] += jnp.dot(act, w2_ref[0].astype(BF16), preferred_element_type=F32)


def moe_ffn_blocks(xs, block_e, n_used, w1, b1, w2, b2, *, block, tf=256):
    rows, d = xs.shape
    nblk = rows // block
    ff = w2.shape[1]
    nf = ff // tf
    sel = (jnp.arange(2 * tf)[:, None] == 2 * jnp.arange(tf)[None, :]).astype(BF16)

    def fe(i, f, nu):
        return jnp.where(i < nu[0], f, nf - 1)

    return pl.pallas_call(
        _ffn_kernel,
        grid_spec=pltpu.PrefetchScalarGridSpec(
            num_scalar_prefetch=2,
            grid=(nblk, nf),
            in_specs=[pl.BlockSpec((block, d), lambda i, f, be, nu: (jnp.minimum(i, nu[0] - 1), 0)),
                      pl.BlockSpec((1, d, 2 * tf), lambda i, f, be, nu: (be[i], 0, fe(i, f, nu))),
                      pl.BlockSpec((1, 1, 2 * tf), lambda i, f, be, nu: (be[i], 0, fe(i, f, nu))),
                      pl.BlockSpec((1, tf, d), lambda i, f, be, nu: (be[i], fe(i, f, nu), 0)),
                      pl.BlockSpec((1, 1, d), lambda i, f, be, nu: (be[i], 0, 0)),
                      pl.BlockSpec((2 * tf, tf), lambda i, f, be, nu: (0, 0))],
            out_specs=pl.BlockSpec((block, d), lambda i, f, be, nu: (i, 0))),
        out_shape=jax.ShapeDtypeStruct((rows, d), F32),
        compiler_params=_cparams(2),
        name="moe_ffn",
    )(block_e, n_used, xs, w1, b1, w2, b2, sel)


def _combine_ln_kernel(slot_ref, nslot_ref, ys_hbm, gate_ref, r_ref, g_ref, b_ref, o_ref, ob_ref, buf_ref, sems):
    tc = r_ref.shape[0]
    i = pl.program_id(0)
    cur = i % 2

    def row_copy(idx_ref, b, r, k):
        return pltpu.make_async_copy(ys_hbm.at[pl.ds(idx_ref[0, 0, r * TOP_K + k], 1)],
                                     buf_ref.at[b, k, pl.ds(r, 1)], sems.at[b])

    def gather(idx_ref, b):
        def issue(r, _):
            for k in range(TOP_K):
                row_copy(idx_ref, b, r, k).start()
            return 0
        lax.fori_loop(0, tc, issue, 0)

    @pl.when(i == 0)
    def _():
        gather(slot_ref, 0)

    @pl.when(i + 1 < pl.num_programs(0))
    def _():
        gather(nslot_ref, 1 - cur)

    def drain(r, _):
        row_copy(slot_ref, cur, 0, 0).wait()
        return 0

    lax.fori_loop(0, tc * TOP_K, drain, 0, unroll=16)

    gate = gate_ref[...]
    ffn = gate[:, 0:1] * buf_ref[cur, 0]
    for k in range(1, TOP_K):
        ffn = ffn + gate[:, k:k + 1] * buf_ref[cur, k]
    y = _layer_norm_rows(DN_ALPHA * r_ref[...] + ffn, g_ref[...], b_ref[...])
    o_ref[...] = y
    ob_ref[...] = y.astype(BF16)


def moe_combine_ln(ys, slot, gate, resid, g, b, *, tc=256):
    t, d = resid.shape
    tc = min(tc, t)
    nt = t // tc
    slot3 = slot.reshape(nt, 1, tc * TOP_K)
    return pl.pallas_call(
        _combine_ln_kernel,
        grid_spec=pltpu.PrefetchScalarGridSpec(
            num_scalar_prefetch=0,
            grid=(nt,),
            in_specs=[pl.BlockSpec((1, 1, tc * TOP_K), lambda i: (i, 0, 0), memory_space=pltpu.SMEM),
                      pl.BlockSpec((1, 1, tc * TOP_K), lambda i: (jnp.minimum(i + 1, nt - 1), 0, 0),
                                   memory_space=pltpu.SMEM),
                      pl.BlockSpec(memory_space=pl.ANY),
                      pl.BlockSpec((tc, LANES), lambda i: (i, 0)),
                      pl.BlockSpec((tc, d), lambda i: (i, 0)),
                      pl.BlockSpec((1, d), lambda i: (0, 0)),
                      pl.BlockSpec((1, d), lambda i: (0, 0))],
            out_specs=[pl.BlockSpec((tc, d), lambda i: (i, 0)),
                       pl.BlockSpec((tc, d), lambda i: (i, 0))],
            scratch_shapes=[pltpu.VMEM((2, TOP_K, tc, d), F32),
                            pltpu.SemaphoreType.DMA((2,))]),
        out_shape=[jax.ShapeDtypeStruct((t, d), F32), jax.ShapeDtypeStruct((t, d), BF16)],
        compiler_params=_cparams(1),
        name="moe_combine_ln",
    )(slot3, slot3, ys, gate, resid, g.astype(F32).reshape(1, d), b.astype(F32).reshape(1, d))


MOE_ROWS_BLOCK = 1024


def moe_layer(h, w_router, b_router, w1, b1, w2, b2, ln_g, ln_b):
    t, d = h.shape
    block = MOE_ROWS_BLOCK
    gate, eidx, rank, counts = moe_router(h, w_router, b_router)
    cnt = counts[0, :N_EXPERTS]
    padded = (cnt + block - 1) // block * block
    pend = jnp.cumsum(padded)
    pstart = pend - padded
    nblk = -(-(t * TOP_K) // block) + N_EXPERTS
    rows = nblk * block
    n_used = (pend[-1] // block).astype(I32).reshape(1)
    blk_first_row = jnp.minimum(jnp.arange(nblk, dtype=I32), n_used - 1) * block
    block_e = jnp.minimum(jnp.sum((blk_first_row[:, None] >= pend[None, :]).astype(I32), axis=1),
                          N_EXPERTS - 1).astype(I32)
    pad_lo = jnp.where(cnt > 0, pend - block, -1).astype(I32)
    e4 = eidx[:, :TOP_K]
    slot = (jnp.sum(jnp.where(e4[:, :, None] == jnp.arange(N_EXPERTS, dtype=I32)[None, None, :],
                              pstart.astype(I32)[None, None, :], 0), axis=-1)
            + rank[:, :TOP_K]).astype(I32)
    xs = moe_dispatch(h, slot, pad_lo, n_used, rows, block)
    ys = moe_ffn_blocks(xs, block_e, n_used, w1.astype(F32), b1.astype(F32).reshape(N_EXPERTS, 1, -1),
                        w2.astype(F32), b2.astype(F32).reshape(N_EXPERTS, 1, d), block=block)
    return moe_combine_ln(ys, slot, gate, h, ln_g, ln_b)


def hybrid_layer(h, hb, bsz, seq, w_in, conv_w, conv_b, dt_bias, a_log, d_skip, ssm_norm, w_out, ln_g, ln_b):
    z = matmul(hb, w_in, n_cols=SSD_WIDTH, col_off=0)
    xbc = matmul(hb, w_in, n_cols=CONV_CH, col_off=SSD_WIDTH)
    dt_off = SSD_WIDTH + CONV_CH
    dt = matmul(hb, w_in, n_cols=LANES, col_off=dt_off)
    qkv = matmul(hb, w_in, n_cols=3 * MOBA_WIDTH, col_off=dt_off + SSD_HEADS)
    y_ssd = ssd_mixer(z, xbc, dt, conv_w, conv_b, dt_bias, a_log, d_skip, ssm_norm, bsz, seq)
    y_att = moba_mixer(qkv, bsz, seq)
    return proj_residual_ln([y_ssd, y_att], w_out.astype(BF16), h, ln_g, ln_b)


def gla_layer(h, hb, bsz, seq, w_in, w_gate2, b_gate, head_norm, w_out, ln_g, ln_b):
    n_main = 2 * GLA_KEY_DIM + 2 * GLA_VAL_DIM
    proj = matmul(hb, w_in, n_cols=n_main, col_off=0)
    gl = matmul(hb, w_in, n_cols=LANES, col_off=n_main)
    o = gla_core(proj, gl, w_gate2, b_gate, head_norm, bsz, seq)
    return proj_residual_ln([o], w_out.astype(BF16), h, ln_g, ln_b)


def kernel(x, hyb_w_in, hyb_conv_w, hyb_conv_b, hyb_dt_bias, hyb_a_log, hyb_d, hyb_norm, hyb_w_out, gla_w_in, gla_w_gate2, gla_b_gate, gla_norm, gla_w_out, ln1_g, ln1_b, ln2_g, ln2_b, moe_w_router, moe_b_router, moe_w1, moe_b1, moe_w2, moe_b2):
    bsz, seq, d = x.shape
    h = x.reshape(bsz * seq, d).astype(F32)
    hb = h.astype(BF16)
    for layer in range(DEPTH):
        j = layer // 2
        if layer % 2 == 0:
            h, hb = hybrid_layer(h, hb, bsz, seq, hyb_w_in[j], hyb_conv_w[j], hyb_conv_b[j], hyb_dt_bias[j],
                                 hyb_a_log[j], hyb_d[j], hyb_norm[j], hyb_w_out[j], ln1_g[layer], ln1_b[layer])
        else:
            h, hb = gla_layer(h, hb, bsz, seq, gla_w_in[j], gla_w_gate2[j], gla_b_gate[j], gla_norm[j],
                              gla_w_out[j], ln1_g[layer], ln1_b[layer])
        h, hb = moe_layer(h, moe_w_router[layer], moe_b_router[layer], moe_w1[layer], moe_b1[layer],
                          moe_w2[layer], moe_b2[layer], ln2_g[layer], ln2_b[layer])
    return h.reshape(bsz, seq, d).astype(x.dtype)
```
